```python
import math
import jax
import jax.numpy as jnp
from jax import lax
import numpy as np

D_MODEL = 1024
BATCH = 32
SEQ = 256
DEPTH = 4
DEC_BATCH = 8
DEC_SEQ = 4096
PAST_LEN = 512

GRID_W = 64
N_MIXERS = 4
EPS = 1e-6
ROPE_THETA = 10000.0

A_HEADS = 16
A_KV_HEADS = 4
A_GROUP = A_HEADS // A_KV_HEADS
A_HD = 64
A_WINDOW = 128
A_BLOCK = 128
A_SCALE = A_HD ** -0.5

SSD_INNER = 2 * D_MODEL
SSD_HD = 64
SSD_HEADS = SSD_INNER // SSD_HD
SSD_STATE = 128
SSD_GROUPS = 4
SSD_CONV_W = 5
SSD_CHUNK = 128
SSD_XBC = SSD_INNER + 2 * SSD_GROUPS * SSD_STATE

DIFF_HEADS = 8
DIFF_HD = 64
DIFF_VD = 2 * DIFF_HD
DIFF_QBLOCK = 128
DIFF_LAYER = 2
DIFF_LAMBDA_INIT = 0.8 - 0.6 * math.exp(-0.3 * DIFF_LAYER)
DIFF_SCALE = DIFF_HD ** -0.5

ML_INNER = 2 * D_MODEL
ML_HEADS = 4
ML_IN_HD = ML_INNER // ML_HEADS
ML_DK = ML_IN_HD // 2
ML_DV = ML_IN_HD
ML_CONV_W = 5
ML_CHUNK = 128

D_FF = 2816
FFN_CONV_W = 3

kernel_name = 'hybrid_diffusion_trunk_step'


def rmsnorm(x, g):
    xf = x.astype(jnp.float32)
    xf = xf * lax.rsqrt(jnp.mean(xf * xf, axis=-1, keepdims=True) + EPS)
    return (xf * g.astype(jnp.float32)).astype(x.dtype)


def softmax_f32(s):
    return jax.nn.softmax(s.astype(jnp.float32), axis=-1)


def dwconv(x, w, b):
    width, ch = w.shape
    pad = width // 2
    y = lax.conv_general_dilated(x, w[:, None, :].astype(x.dtype), (1,), [(pad, pad)],
                                 dimension_numbers=('NWC', 'WIO', 'NWC'), feature_group_count=ch)
    return y + b.astype(x.dtype)


def flip(t):
    return jnp.flip(t, axis=1)


def to_chunks(t, q):
    b, L = t.shape[:2]
    return jnp.moveaxis(t.reshape(b, L // q, q, *t.shape[2:]), 1, 0)


def from_chunks(t):
    nc, b, q = t.shape[:3]
    return jnp.moveaxis(t, 0, 1).reshape(b, nc * q, *t.shape[3:])


def axial_rope(L, hd):
    rows = L // GRID_W
    row = jnp.repeat(jnp.arange(rows, dtype=jnp.float32), GRID_W)
    col = jnp.tile(jnp.arange(GRID_W, dtype=jnp.float32), rows)
    nf = hd // 4
    inv = ROPE_THETA ** (-jnp.arange(nf, dtype=jnp.float32) / nf)
    ang = jnp.concatenate([row[:, None] * inv, col[:, None] * inv], axis=-1)
    return jnp.cos(ang), jnp.sin(ang)


def apply_rope(x, cos, sin):
    half = x.shape[-1] // 2
    shp = (1, cos.shape[0]) + (1,) * (x.ndim - 3) + (half,)
    c = cos.reshape(shp).astype(x.dtype)
    s = sin.reshape(shp).astype(x.dtype)
    x1, x2 = x[..., :half], x[..., half:]
    return jnp.concatenate([x1 * c - x2 * s, x1 * s + x2 * c], axis=-1)


def attn_project(xn, w_qkv, g_q, g_k):
    b, L, _ = xn.shape
    q, k, v = jnp.split(xn @ w_qkv, [A_HEADS * A_HD, (A_HEADS + A_KV_HEADS) * A_HD], axis=-1)
    q = rmsnorm(q.reshape(b, L, A_KV_HEADS, A_GROUP, A_HD), g_q)
    k = rmsnorm(k.reshape(b, L, A_KV_HEADS, A_HD), g_k)
    v = v.reshape(b, L, A_KV_HEADS, A_HD)
    return q, k, v


def attn_ctx(xn, w_qkv, g_q, g_k, sink, w_o):
    b, L, _ = xn.shape
    q, k, v = attn_project(xn, w_qkv, g_q, g_k)
    s = jnp.einsum('blgrd,bmgd->bgrlm', q, k).astype(jnp.float32) * A_SCALE
    sk = jnp.broadcast_to(sink.astype(jnp.float32).reshape(1, A_KV_HEADS, A_GROUP, 1, 1),
                          (b, A_KV_HEADS, A_GROUP, L, 1))
    p = softmax_f32(jnp.concatenate([s, sk], axis=-1))[..., :L].astype(v.dtype)
    o = jnp.einsum('bgrlm,bmgd->blgrd', p, v).reshape(b, L, A_HEADS * A_HD)
    return o @ w_o, k, v


def attn_lat(xn, ck, cv, w_qkv, g_q, g_k, sink, w_o):
    b, L, _ = xn.shape
    nb = L // A_BLOCK
    q, k, v = attn_project(xn, w_qkv, g_q, g_k)
    cos, sin = axial_rope(L, A_HD)
    q = apply_rope(q, cos, sin)
    k = apply_rope(k, cos, sin)
    qb = q.reshape(b, nb, A_BLOCK, A_KV_HEADS, A_GROUP, A_HD)

    def band(t):
        tp = jnp.pad(t, ((0, 0), (A_BLOCK, A_BLOCK), (0, 0), (0, 0)))
        tp = tp.reshape(b, nb + 2, A_BLOCK, A_KV_HEADS, A_HD)
        return jnp.concatenate([tp[:, :-2], tp[:, 1:-1], tp[:, 2:]], axis=2)

    kw, vw = band(k), band(v)
    s_loc = jnp.einsum('bnqgrd,bnkgd->bngrqk', qb, kw).astype(jnp.float32) * A_SCALE
    qpos = jnp.arange(nb)[:, None, None] * A_BLOCK + jnp.arange(A_BLOCK)[None, :, None]
    kpos = jnp.arange(nb)[:, None, None] * A_BLOCK - A_BLOCK + jnp.arange(3 * A_BLOCK)[None, None, :]
    mask = (jnp.abs(qpos - kpos) <= A_WINDOW) & (kpos >= 0) & (kpos < L)
    s_loc = jnp.where(mask[None, :, None, None], s_loc, -jnp.inf)
    s_ctx = jnp.einsum('bnqgrd,bmgd->bngrqm', qb, ck).astype(jnp.float32) * A_SCALE
    sk = jnp.broadcast_to(sink.astype(jnp.float32).reshape(1, 1, A_KV_HEADS, A_GROUP, 1, 1),
                          (b, nb, A_KV_HEADS, A_GROUP, A_BLOCK, 1))
    p = softmax_f32(jnp.concatenate([s_loc, s_ctx, sk], axis=-1)).astype(v.dtype)
    w3 = 3 * A_BLOCK
    o = (jnp.einsum('bngrqk,bnkgd->bnqgrd', p[..., :w3], vw)
         + jnp.einsum('bngrqm,bmgd->bnqgrd', p[..., w3:-1], cv.astype(v.dtype)))
    return o.reshape(b, L, A_HEADS * A_HD) @ w_o


def ssd_scan(x, dt, A, bm, cm, h0):
    b, L = x.shape[:2]
    R = SSD_HEADS // SSD_GROUPS
    xs = (to_chunks(x.astype(jnp.float32).reshape(b, L, SSD_GROUPS, R, SSD_HD), SSD_CHUNK),
          to_chunks(dt.reshape(b, L, SSD_GROUPS, R), SSD_CHUNK),
          to_chunks(bm.astype(jnp.float32), SSD_CHUNK),
          to_chunks(cm.astype(jnp.float32), SSD_CHUNK))
    A = A.reshape(SSD_GROUPS, R)
    causal = jnp.tril(jnp.ones((SSD_CHUNK, SSD_CHUNK), dtype=bool))

    def body(h, inp):
        xc, dtc, bc, cc = inp
        acs = jnp.cumsum(dtc * A, axis=1)
        seg = jnp.where(causal[None, :, :, None, None], acs[:, :, None] - acs[:, None, :], -jnp.inf)
        xdt = xc * dtc[..., None]
        cb = jnp.einsum('bign,bjgn->bijg', cc, bc)
        y = (jnp.einsum('bijg,bijgr,bjgrp->bigrp', cb, jnp.exp(seg), xdt)
             + jnp.einsum('bign,bgrpn->bigrp', cc, h) * jnp.exp(acs)[..., None])
        last = acs[:, -1]
        h = (h * jnp.exp(last)[..., None, None]
             + jnp.einsum('bjgr,bjgrp,bjgn->bgrpn', jnp.exp(last[:, None] - acs), xdt, bc))
        return h, y

    hT, ys = lax.scan(body, h0.astype(jnp.float32).reshape(b, SSD_GROUPS, R, SSD_HD, SSD_STATE), xs)
    y = from_chunks(ys).reshape(b, L, SSD_HEADS, SSD_HD).astype(x.dtype)
    return y, hT.reshape(b, SSD_HEADS, SSD_HD, SSD_STATE).astype(x.dtype)


def ssd_mixer(xn, h0, w_in, conv_w, conv_b, dt_bias, a_log, d_skip, g_norm, w_out):
    b, L, _ = xn.shape
    z, xbc, dt = jnp.split(xn @ w_in, [SSD_INNER, SSD_INNER + SSD_XBC], axis=-1)
    xbc = jax.nn.silu(dwconv(xbc, conv_w, conv_b))
    xs, bm, cm = jnp.split(xbc, [SSD_INNER, SSD_INNER + SSD_GROUPS * SSD_STATE], axis=-1)
    xs = xs.reshape(b, L, SSD_HEADS, SSD_HD)
    bm = bm.reshape(b, L, SSD_GROUPS, SSD_STATE)
    cm = cm.reshape(b, L, SSD_GROUPS, SSD_STATE)
    dt = jax.nn.softplus(dt.astype(jnp.float32).reshape(b, L, 2, SSD_HEADS) + dt_bias.astype(jnp.float32))
    A = -jnp.exp(a_log.astype(jnp.float32))
    yf, hf = ssd_scan(xs, dt[:, :, 0], A[0], bm, cm, h0[:, 0])
    yb, hb = ssd_scan(flip(xs), flip(dt[:, :, 1]), A[1], flip(bm), flip(cm), h0[:, 1])
    y = yf + flip(yb) + xs * d_skip[:, None]
    y = rmsnorm(y.reshape(b, L, SSD_INNER) * jax.nn.silu(z), g_norm)
    return y @ w_out, jnp.stack([hf, hb], axis=1)


def diff_project(xn, w_qkv, g_q, g_k):
    b, L, _ = xn.shape
    nq = DIFF_HEADS * 2 * DIFF_HD
    q, k, v = jnp.split(xn @ w_qkv, [nq, 2 * nq], axis=-1)
    q = rmsnorm(q.reshape(b, L, DIFF_HEADS, 2, DIFF_HD), g_q)
    k = rmsnorm(k.reshape(b, L, DIFF_HEADS, 2, DIFF_HD), g_k)
    v = v.reshape(b, L, DIFF_HEADS, DIFF_VD)
    return q, k, v


def diff_lambda(lq1, lk1, lq2, lk2):
    f = lambda a, c: jnp.exp(jnp.sum(a.astype(jnp.float32) * c.astype(jnp.float32)))
    return f(lq1, lk1) - f(lq2, lk2) + DIFF_LAMBDA_INIT


def diff_out(o, g_sub, w_o):
    b, L = o.shape[:2]
    o = rmsnorm(o, g_sub) * (1.0 - DIFF_LAMBDA_INIT)
    return o.reshape(b, L, DIFF_HEADS * DIFF_VD) @ w_o


def diff_ctx(xn, w_qkv, g_q, g_k, lq1, lk1, lq2, lk2, g_sub, w_o):
    q, k, v = diff_project(xn, w_qkv, g_q, g_k)
    lam = diff_lambda(lq1, lk1, lq2, lk2)
    p = softmax_f32(jnp.einsum('blhcd,bmhcd->bhclm', q, k).astype(jnp.float32) * DIFF_SCALE)
    a = (p[:, :, 0] - lam * p[:, :, 1]).astype(v.dtype)
    o = jnp.einsum('bhlm,bmhv->blhv', a, v)
    return diff_out(o, g_sub, w_o), k, v


def diff_lat(xn, ck, cv, w_qkv, g_q, g_k, lq1, lk1, lq2, lk2, g_sub, w_o):
    b, L, _ = xn.shape
    nb = L // DIFF_QBLOCK
    q, k, v = diff_project(xn, w_qkv, g_q, g_k)
    cos, sin = axial_rope(L, DIFF_HD)
    q = apply_rope(q, cos, sin)
    k = apply_rope(k, cos, sin)
    lam = diff_lambda(lq1, lk1, lq2, lk2)
    ck = ck.astype(k.dtype)
    cv = cv.astype(v.dtype)
    qb = jnp.moveaxis(q.reshape(b, nb, DIFF_QBLOCK, DIFF_HEADS, 2, DIFF_HD), 1, 0)

    def block(qi):
        s = jnp.concatenate([jnp.einsum('bqhcd,bmhcd->bhcqm', qi, k),
                             jnp.einsum('bqhcd,bmhcd->bhcqm', qi, ck)], axis=-1)
        p = softmax_f32(s.astype(jnp.float32) * DIFF_SCALE)
        a = (p[:, :, 0] - lam * p[:, :, 1]).astype(v.dtype)
        return (jnp.einsum('bhqm,bmhv->bqhv', a[..., :L], v)
                + jnp.einsum('bhqm,bmhv->bqhv', a[..., L:], cv))

    o = jnp.moveaxis(lax.map(block, qb), 0, 1).reshape(b, L, DIFF_HEADS, DIFF_VD)
    return diff_out(o, g_sub, w_o)


def mlstm_scan(q, k, v, i_pre, f_pre, C0, n0, m0):
    causal = jnp.tril(jnp.ones((ML_CHUNK, ML_CHUNK), dtype=bool))
    logf = jax.nn.log_sigmoid(f_pre.astype(jnp.float32))
    xs = tuple(to_chunks(t.astype(jnp.float32), ML_CHUNK) for t in (q, k, v, i_pre, logf))

    def body(carry, inp):
        C, n, m = carry
        qc, kc, vc, ic, lfc = inp
        bcum = jnp.cumsum(lfc, axis=1)
        d = jnp.where(causal[None, :, :, None],
                      bcum[:, :, None] - bcum[:, None, :] + ic[:, None, :], -jnp.inf)
        inter = bcum + m[:, None]
        mt = jnp.maximum(jnp.max(d, axis=2), inter)
        s = jnp.einsum('bihd,bjhd->bijh', qc, kc) * jnp.exp(d - mt[:, :, None])
        w_int = jnp.exp(inter - mt)
        num = (jnp.einsum('bijh,bjhv->bihv', s, vc)
               + w_int[..., None] * jnp.einsum('bihd,bhdv->bihv', qc, C))
        den = jnp.sum(s, axis=2) + w_int * jnp.einsum('bihd,bhd->bih', qc, n)
        h = num / jnp.maximum(jnp.abs(den), jnp.exp(-mt))[..., None]
        btot = bcum[:, -1]
        g = btot[:, None] - bcum + ic
        m_new = jnp.maximum(btot + m, jnp.max(g, axis=1))
        wg = jnp.exp(g - m_new[:, None])
        decay = jnp.exp(btot + m - m_new)
        C = C * decay[..., None, None] + jnp.einsum('bjh,bjhd,bjhv->bhdv', wg, kc, vc)
        n = n * decay[..., None] + jnp.einsum('bjh,bjhd->bhd', wg, kc)
        return (C, n, m_new), h

    init = (C0.astype(jnp.float32), n0.astype(jnp.float32), m0.astype(jnp.float32))
    (C, n, m), hs = lax.scan(body, init, xs)
    dt = q.dtype
    return from_chunks(hs).astype(dt), C.astype(dt), n.astype(dt), m.astype(dt)


def mlstm_mixer(xn, C0, n0, m0, w_in, conv_w, conv_b, w_q, w_k, w_v, w_if, b_if, g_norm, w_out):
    b, L, _ = xn.shape
    xm, o_pre = jnp.split(xn @ w_in, 2, axis=-1)
    xc = jax.nn.silu(dwconv(xm, conv_w, conv_b))
    heads = lambda t: t.reshape(b, L, ML_HEADS, ML_IN_HD)
    q = jnp.einsum('blhi,hid->blhd', heads(xc), w_q)
    k = jnp.einsum('blhi,hid->blhd', heads(xc), w_k) * (ML_DK ** -0.5)
    v = jnp.einsum('blhi,hiv->blhv', heads(xm), w_v)
    gates = (xc @ w_if).astype(jnp.float32).reshape(b, L, 2, 2, ML_HEADS) + b_if.astype(jnp.float32)
    hf, Cf, nf, mf = mlstm_scan(q, k, v, gates[:, :, 0, 0], gates[:, :, 0, 1], C0[:, 0], n0[:, 0], m0[:, 0])
    hb, Cb, nb_, mb = mlstm_scan(flip(q), flip(k), flip(v), flip(gates[:, :, 1, 0]), flip(gates[:, :, 1, 1]),
                                 C0[:, 1], n0[:, 1], m0[:, 1])
    h = rmsnorm(hf + flip(hb), g_norm)
    y = jax.nn.sigmoid(o_pre) * h.reshape(b, L, ML_INNER)
    return (y @ w_out, jnp.stack([Cf, Cb], axis=1), jnp.stack([nf, nb_], axis=1),
            jnp.stack([mf, mb], axis=1))


def conv_ffn(xn, w_up, conv_w, conv_b, w_down):
    a, g = jnp.split(dwconv(xn @ w_up, conv_w, conv_b), 2, axis=-1)
    return (a * jax.nn.silu(g)) @ w_down


def setup_inputs(seed: int = 0) -> dict:
    key = jax.random.key(seed)
    keys = iter(jax.random.split(key, 96))
    f32 = jnp.float32

    def nrm(shape, scale):
        return jax.random.normal(next(keys), shape, f32) * scale

    def gain(shape):
        return 1.0 + nrm(shape, 0.05)

    D = D_MODEL
    dt0 = jnp.exp(jax.random.uniform(next(keys), (2, SSD_HEADS), f32, math.log(1e-3), math.log(1e-1)))
    b_i = nrm((2, 1, ML_HEADS), 0.1)
    b_f = jnp.linspace(3.0, 6.0, ML_HEADS, dtype=f32)[None, None, :] + nrm((2, 1, ML_HEADS), 0.1)
    return {
        'x_prompt': nrm((BATCH, SEQ, D), 1.0),
        'x_sample': nrm((DEC_BATCH, DEC_SEQ, D), 1.0),
        'cache_attn_k': nrm((DEC_BATCH, PAST_LEN, A_KV_HEADS, A_HD), 1.0),
        'cache_attn_v': nrm((DEC_BATCH, PAST_LEN, A_KV_HEADS, A_HD), 1.0),
        'state_ssd': nrm((DEC_BATCH, 2, SSD_HEADS, SSD_HD, SSD_STATE), 0.1),
        'cache_diff_k': nrm((DEC_BATCH, PAST_LEN, DIFF_HEADS, 2, DIFF_HD), 1.0),
        'cache_diff_v': nrm((DEC_BATCH, PAST_LEN, DIFF_HEADS, DIFF_VD), 1.0),
        'state_mlstm_c': nrm((DEC_BATCH, 2, ML_HEADS, ML_DK, ML_DV), 0.1),
        'state_mlstm_n': nrm((DEC_BATCH, 2, ML_HEADS, ML_DK), 0.1),
        'state_mlstm_m': nrm((DEC_BATCH, 2, ML_HEADS), 0.5),
        'c': nrm((DEC_BATCH, D), 1.0),
        'c_ctx': nrm((D,), 1.0),
        'ada_w': nrm((DEPTH, D, 6 * D), 0.5 * D ** -0.5),
        'ada_b': nrm((DEPTH, 6 * D), 0.02),
        'norm1_g': gain((DEPTH, D)),
        'norm2_g': gain((DEPTH, D)),
        'ffn_w_up': nrm((DEPTH, D, 2 * D_FF), D ** -0.5),
        'ffn_conv_w': nrm((DEPTH, FFN_CONV_W, 2 * D_FF), FFN_CONV_W ** -0.5),
        'ffn_conv_b': nrm((DEPTH, 2 * D_FF), 0.02),
        'ffn_w_down': nrm((DEPTH, D_FF, D), D_FF ** -0.5),
        'attn_w_qkv': nrm((D, (A_HEADS + 2 * A_KV_HEADS) * A_HD), D ** -0.5),
        'attn_g_q': gain((A_HD,)),
        'attn_g_k': gain((A_HD,)),
        'attn_sink': nrm((A_HEADS,), 0.5),
        'attn_w_o': nrm((A_HEADS * A_HD, D), (A_HEADS * A_HD) ** -0.5),
        'ssd_w_in': nrm((D, SSD_INNER + SSD_XBC + 2 * SSD_HEADS), D ** -0.5),
        'ssd_conv_w': nrm((SSD_CONV_W, SSD_XBC), SSD_CONV_W ** -0.5),
        'ssd_conv_b': nrm((SSD_XBC,), 0.02),
        'ssd_dt_bias': dt0 + jnp.log(-jnp.expm1(-dt0)),
        'ssd_a_log': jnp.log(jax.random.uniform(next(keys), (2, SSD_HEADS), f32, 1.0, 16.0)),
        'ssd_d': gain((SSD_HEADS,)),
        'ssd_g_norm': gain((SSD_INNER,)),
        'ssd_w_out': nrm((SSD_INNER, D), SSD_INNER ** -0.5),
        'diff_w_qkv': nrm((D, DIFF_HEADS * (4 * DIFF_HD + DIFF_VD)), D ** -0.5),
        'diff_g_q': gain((2, DIFF_HD)),
        'diff_g_k': gain((2, DIFF_HD)),
        'diff_lq1': nrm((DIFF_HD,), 0.1),
        'diff_lk1': nrm((DIFF_HD,), 0.1),
        'diff_lq2': nrm((DIFF_HD,), 0.1),
        'diff_lk2': nrm((DIFF_HD,), 0.1),
        'diff_g_sub': gain((DIFF_VD,)),
        'diff_w_o': nrm((DIFF_HEADS * DIFF_VD, D), (DIFF_HEADS * DIFF_VD) ** -0.5),
        'ml_w_in': nrm((D, 2 * ML_INNER), D ** -0.5),
        'ml_conv_w': nrm((ML_CONV_W, ML_INNER), ML_CONV_W ** -0.5),
        'ml_conv_b': nrm((ML_INNER,), 0.02),
        'ml_w_q': nrm((ML_HEADS, ML_IN_HD, ML_DK), ML_IN_HD ** -0.5),
        'ml_w_k': nrm((ML_HEADS, ML_IN_HD, ML_DK), ML_IN_HD ** -0.5),
        'ml_w_v': nrm((ML_HEADS, ML_IN_HD, ML_DV), ML_IN_HD ** -0.5),
        'ml_w_if': nrm((ML_INNER, 4 * ML_HEADS), ML_INNER ** -0.5),
        'ml_b_if': jnp.concatenate([b_i, b_f], axis=1),
        'ml_g_norm': gain((ML_HEADS, ML_DV)),
        'ml_w_out': nrm((ML_INNER, D), ML_INNER ** -0.5),
    }


def reference(x_prompt, x_sample, cache_attn_k, cache_attn_v, state_ssd, cache_diff_k, cache_diff_v,
              state_mlstm_c, state_mlstm_n, state_mlstm_m, c, c_ctx,
              ada_w, ada_b, norm1_g, norm2_g, ffn_w_up, ffn_conv_w, ffn_conv_b, ffn_w_down,
              attn_w_qkv, attn_g_q, attn_g_k, attn_sink, attn_w_o,
              ssd_w_in, ssd_conv_w, ssd_conv_b, ssd_dt_bias, ssd_a_log, ssd_d, ssd_g_norm, ssd_w_out,
              diff_w_qkv, diff_g_q, diff_g_k, diff_lq1, diff_lk1, diff_lq2, diff_lk2, diff_g_sub, diff_w_o,
              ml_w_in, ml_conv_w, ml_conv_b, ml_w_q, ml_w_k, ml_w_v, ml_w_if, ml_b_if, ml_g_norm, ml_w_out):
    xp, xs = x_prompt, x_sample
    bp = xp.shape[0]
    for i in range(DEPTH):
        kind = i % N_MIXERS
        mp = (jax.nn.silu(c_ctx) @ ada_w[i] + ada_b[i]).reshape(6, D_MODEL)
        ms = (jax.nn.silu(c) @ ada_w[i] + ada_b[i]).reshape(-1, 6, 1, D_MODEL)
        hp = rmsnorm(xp, norm1_g[i]) * (1 + mp[1]) + mp[0]
        hs = rmsnorm(xs, norm1_g[i]) * (1 + ms[:, 1]) + ms[:, 0]
        if kind == 0:
            op, new_attn_k, new_attn_v = attn_ctx(hp, attn_w_qkv, attn_g_q, attn_g_k, attn_sink, attn_w_o)
            osm = attn_lat(hs, cache_attn_k, cache_attn_v, attn_w_qkv, attn_g_q, attn_g_k, attn_sink, attn_w_o)
        elif kind == 1:
            h0 = jnp.zeros((bp, 2, SSD_HEADS, SSD_HD, SSD_STATE), xp.dtype)
            op, new_ssd = ssd_mixer(hp, h0, ssd_w_in, ssd_conv_w, ssd_conv_b, ssd_dt_bias, ssd_a_log,
                                    ssd_d, ssd_g_norm, ssd_w_out)
            osm, _ = ssd_mixer(hs, state_ssd, ssd_w_in, ssd_conv_w, ssd_conv_b, ssd_dt_bias, ssd_a_log,
                               ssd_d, ssd_g_norm, ssd_w_out)
        elif kind == 2:
            op, new_diff_k, new_diff_v = diff_ctx(hp, diff_w_qkv, diff_g_q, diff_g_k, diff_lq1, diff_lk1,
                                                  diff_lq2, diff_lk2, diff_g_sub, diff_w_o)
            osm = diff_lat(hs, cache_diff_k, cache_diff_v, diff_w_qkv, diff_g_q, diff_g_k, diff_lq1, diff_lk1,
                           diff_lq2, diff_lk2, diff_g_sub, diff_w_o)
        else:
            c0 = jnp.zeros((bp, 2, ML_HEADS, ML_DK, ML_DV), xp.dtype)
            n0 = jnp.zeros((bp, 2, ML_HEADS, ML_DK), xp.dtype)
            m0 = jnp.zeros((bp, 2, ML_HEADS), xp.dtype)
            op, new_ml_c, new_ml_n, new_ml_m = mlstm_mixer(hp, c0, n0, m0, ml_w_in, ml_conv_w, ml_conv_b,
                                                           ml_w_q, ml_w_k, ml_w_v, ml_w_if, ml_b_if,
                                                           ml_g_norm, ml_w_out)
            osm, _, _, _ = mlstm_mixer(hs, state_mlstm_c, state_mlstm_n, state_mlstm_m, ml_w_in, ml_conv_w,
                                       ml_conv_b, ml_w_q, ml_w_k, ml_w_v, ml_w_if, ml_b_if, ml_g_norm, ml_w_out)
        xp = xp + mp[2] * op
        xs = xs + ms[:, 2] * osm
        hp = rmsnorm(xp, norm2_g[i]) * (1 + mp[4]) + mp[3]
        hs = rmsnorm(xs, norm2_g[i]) * (1 + ms[:, 4]) + ms[:, 3]
        xp = xp + mp[5] * conv_ffn(hp, ffn_w_up[i], ffn_conv_w[i], ffn_conv_b[i], ffn_w_down[i])
        xs = xs + ms[:, 5] * conv_ffn(hs, ffn_w_up[i], ffn_conv_w[i], ffn_conv_b[i], ffn_w_down[i])
    return (xp, xs, new_attn_k, new_attn_v, new_ssd, new_diff_k, new_diff_v, new_ml_c, new_ml_n, new_ml_m)
```

```python
import functools
import math

import jax
import jax.numpy as jnp
from jax import lax
from jax.experimental import pallas as pl
from jax.experimental.pallas import tpu as pltpu

F32 = jnp.float32
BF16 = jnp.bfloat16
HIGHEST = lax.Precision.HIGHEST

EPS = 1e-6
ROPE_THETA = 10000.0
GRID_W = 64
DIFF_LAMBDA_INIT = 0.8 - 0.6 * math.exp(-0.3 * 2)

VMEM_LIMIT_BYTES = 56 * 1024 * 1024
LANES = 128
HALO = 16
CHUNK = 128
NEG_INF = float("-inf")


def _cparams(*sem):
    return pltpu.CompilerParams(dimension_semantics=sem, vmem_limit_bytes=VMEM_LIMIT_BYTES)


def _dot(a, b):
    return jnp.dot(a, b, preferred_element_type=F32)


def _dot_nt(a, b):
    return lax.dot_general(a, b, (((1,), (1,)), ((), ())), preferred_element_type=F32)


def _dot_exact(a, b):
    return jnp.dot(a, b, preferred_element_type=F32, precision=HIGHEST)


def _norm_mod(x, a, sh):
    ms = jnp.mean(x * x, axis=-1, keepdims=True)
    return x * lax.rsqrt(ms + EPS) * a + sh


def _silu(x):
    return x * jax.nn.sigmoid(x)


def _softplus(x):
    return jnp.maximum(x, 0.0) + jnp.log(1.0 + jnp.exp(-jnp.abs(x)))


def _log_sigmoid(x):
    return jnp.minimum(x, 0.0) - jnp.log(1.0 + jnp.exp(-jnp.abs(x)))


def _const_spec(shape):
    nd = len(shape)
    return pl.BlockSpec(shape, lambda *_: (0,) * nd)


def _ext_rows(x_ref, xp_ref, xn_ref, a, sh, nl):
    l = pl.program_id(1)
    xt = _norm_mod(x_ref[0], a, sh)
    xp = _norm_mod(xp_ref[0], a, sh) * (l > 0).astype(F32)
    xn = _norm_mod(xn_ref[0], a, sh) * (l < nl - 1).astype(F32)
    return jnp.concatenate([xp, xt, xn], axis=0)


def _halo_specs(tm, d, nl):
    r = tm // HALO
    return [
        pl.BlockSpec((1, tm, d), lambda b, l: (b, l, 0)),
        pl.BlockSpec((1, HALO, d), lambda b, l: (b, jnp.maximum(l * r - 1, 0), 0)),
        pl.BlockSpec((1, HALO, d), lambda b, l: (b, jnp.minimum((l + 1) * r, nl * r - 1), 0)),
    ]


def _dwconv_rows(h, w, b, width):
    rows = h.shape[0]
    pad = width // 2
    y = h * w[pad:pad + 1]
    for k in range(width):
        if k == pad:
            continue
        y = y + pltpu.roll(h, (pad - k) % rows, 0) * w[k:k + 1]
    return y + b


def _ada_kernel(c_ref, w_ref, b_ref, o_ref):
    s = _silu(c_ref[...]).astype(BF16)
    o_ref[0] = _dot(s, w_ref[0].astype(BF16)) + b_ref[0]


def _ada(cond, ada_w, ada_b):
    depth, d, n = ada_w.shape
    r = cond.shape[0]
    tn = 1536
    return pl.pallas_call(
        _ada_kernel,
        out_shape=jax.ShapeDtypeStruct((depth, r, n), F32),
        grid=(depth, n // tn),
        in_specs=[
            pl.BlockSpec((r, d), lambda i, j: (0, 0)),
            pl.BlockSpec((1, d, tn), lambda i, j: (i, 0, j)),
            pl.BlockSpec((1, 1, tn), lambda i, j: (i, 0, j)),
        ],
        out_specs=pl.BlockSpec((1, r, tn), lambda i, j: (i, 0, j)),
        compiler_params=_cparams("parallel", "parallel"),
        name="ada",
    )(cond, ada_w, ada_b.reshape(depth, 1, n))


def _ffn_kernel(x_ref, xp_ref, xn_ref, a_ref, sh_ref, gt_ref, wa_ref, wg_ref, cwa_ref, cwg_ref, cba_ref, cbg_ref,
                wd_ref, o_ref, xe_ref, acc_ref, *, tm, nl, nchunk):
    xe_ref[...] = _ext_rows(x_ref, xp_ref, xn_ref, a_ref[0], sh_ref[0], nl).astype(BF16)
    acc_ref[...] = jnp.zeros_like(acc_ref)

    def body(j, carry):
        xe = xe_ref[...]
        ha = _dwconv_rows(_dot(xe, wa_ref[j]), cwa_ref[j], cba_ref[j], 3)[HALO:HALO + tm]
        hg = _dwconv_rows(_dot(xe, wg_ref[j]), cwg_ref[j], cbg_ref[j], 3)[HALO:HALO + tm]
        u = ha * _silu(hg)
        acc_ref[...] += _dot(u.astype(BF16), wd_ref[j])
        return carry

    lax.fori_loop(0, nchunk, body, 0)
    o_ref[0] = x_ref[0] + gt_ref[0] * acc_ref[...]


def _ffn(x, a, sh, gt, w_up, conv_w, conv_b, w_down, tm):
    bsz, seq, d = x.shape
    dff = w_down.shape[0]
    cn = 256
    nchunk = dff // cn
    nl = seq // tm
    wa = w_up[:, :dff].reshape(d, nchunk, cn).transpose(1, 0, 2).astype(BF16)
    wg = w_up[:, dff:].reshape(d, nchunk, cn).transpose(1, 0, 2).astype(BF16)
    cwa = conv_w[:, :dff].reshape(3, nchunk, cn).transpose(1, 0, 2)
    cwg = conv_w[:, dff:].reshape(3, nchunk, cn).transpose(1, 0, 2)
    cba = conv_b[:dff].reshape(nchunk, 1, cn)
    cbg = conv_b[dff:].reshape(nchunk, 1, cn)
    wd = w_down.reshape(nchunk, cn, d).astype(BF16)
    mod_spec = pl.BlockSpec((1, 1, d), lambda b, l: (b, 0, 0))
    return pl.pallas_call(
        functools.partial(_ffn_kernel, tm=tm, nl=nl, nchunk=nchunk),
        out_shape=jax.ShapeDtypeStruct(x.shape, F32),
        grid=(bsz, nl),
        in_specs=_halo_specs(tm, d, nl) + [mod_spec, mod_spec, mod_spec,
                                       _const_spec(wa.shape), _const_spec(wg.shape), _const_spec(cwa.shape),
                                       _const_spec(cwg.shape), _const_spec(cba.shape), _const_spec(cbg.shape),
                                       _const_spec(wd.shape)],
        out_specs=pl.BlockSpec((1, tm, d), lambda b, l: (b, l, 0)),
        scratch_shapes=[pltpu.VMEM((tm + 2 * HALO, d), BF16), pltpu.VMEM((tm, d), F32)],
        compiler_params=_cparams("parallel", "parallel"),
        name="ffn",
    )(x, x, x, a, sh, gt, wa, wg, cwa, cwg, cba, cbg, wd)


def _oproj_kernel(*refs, prologue, n_in):
    in_refs = refs[:n_in]
    x_ref, gt_ref, w_ref, o_ref = refs[n_in:]
    lhs = prologue(*in_refs)
    o_ref[0] = x_ref[0] + gt_ref[0] * _dot(lhs.astype(BF16), w_ref[...])


def _oproj(prologue, ins, in_specs, x, gt, w, tm):
    bsz, seq, d = x.shape
    return pl.pallas_call(
        functools.partial(_oproj_kernel, prologue=prologue, n_in=len(ins)),
        out_shape=jax.ShapeDtypeStruct(x.shape, F32),
        grid=(bsz, seq // tm),
        in_specs=list(in_specs) + [pl.BlockSpec((1, tm, d), lambda b, l: (b, l, 0)),
                                   pl.BlockSpec((1, 1, d), lambda b, l: (b, 0, 0)),
                                   _const_spec(w.shape)],
        out_specs=pl.BlockSpec((1, tm, d), lambda b, l: (b, l, 0)),
        compiler_params=_cparams("parallel", "parallel"),
        name="oproj",
    )(*ins, x, gt, w.astype(BF16))


def _row_spec(tm, n):
    return pl.BlockSpec((1, tm, n), lambda b, l: (b, l, 0))


def _head_sumsq_matrix(hd):
    i = lax.broadcasted_iota(jnp.int32, (LANES, LANES), 0) // hd
    j = lax.broadcasted_iota(jnp.int32, (LANES, LANES), 1) // hd
    return jnp.where(i == j, 1.0 / hd, 0.0).astype(BF16)


def _qk_norm_rope(y, gain, rope, avg):
    outs = []
    for c in range(y.shape[1] // LANES):
        yb = y[:, c * LANES:(c + 1) * LANES]
        ms = _dot((yb * yb).astype(BF16), avg)
        yb = yb * lax.rsqrt(ms + EPS) * gain
        if rope is not None:
            cos, s_lo, s_hi = rope
            yb = yb * cos + pltpu.roll(yb, LANES - 32, 1) * s_lo + pltpu.roll(yb, 32, 1) * s_hi
        outs.append(yb)
    return outs


def _rope_tables(seq, hd):
    rows = seq // GRID_W
    row = jnp.repeat(jnp.arange(rows, dtype=F32), GRID_W)
    col = jnp.tile(jnp.arange(GRID_W, dtype=F32), rows)
    nf = hd // 4
    inv = ROPE_THETA ** (-jnp.arange(nf, dtype=F32) / nf)
    ang = jnp.concatenate([row[:, None] * inv, col[:, None] * inv], axis=-1)
    cos, sin = jnp.cos(ang), jnp.sin(ang)
    zero = jnp.zeros_like(sin)
    rep = LANES // hd
    cos_t = jnp.tile(jnp.concatenate([cos, cos], axis=-1), (1, rep))
    s_lo = jnp.tile(jnp.concatenate([-sin, zero], axis=-1), (1, rep))
    s_hi = jnp.tile(jnp.concatenate([zero, sin], axis=-1), (1, rep))
    return cos_t, s_lo, s_hi


def _attn_qkv_kernel(*refs, use_rope, nq, nk):
    if use_rope:
        x_ref, a_ref, sh_ref, w_ref, gq_ref, gk_ref, cos_ref, slo_ref, shi_ref, q_ref, k_ref, v_ref = refs
        rope = (cos_ref[...], slo_ref[...], shi_ref[...])
    else:
        x_ref, a_ref, sh_ref, w_ref, gq_ref, gk_ref, q_ref, k_ref, v_ref = refs
        rope = None
    xn = _norm_mod(x_ref[0], a_ref[0], sh_ref[0]).astype(BF16)
    y = _dot(xn, w_ref[...])
    avg = _head_sumsq_matrix(64)
    q = _qk_norm_rope(y[:, :nq], gq_ref[...], rope, avg)
    k = _qk_norm_rope(y[:, nq:nq + nk], gk_ref[...], rope, avg)
    for c, blk in enumerate(q):
        q_ref[0, :, c * LANES:(c + 1) * LANES] = (blk * 0.125).astype(q_ref.dtype)
    for c, blk in enumerate(k):
        k_ref[0, :, c * LANES:(c + 1) * LANES] = blk.astype(k_ref.dtype)
    v_ref[0] = y[:, nq + nk:].astype(v_ref.dtype)


def _qkv_project(x, a, sh, w, gq, gk, nq, nk, nv, rope, kv_dtype, tm):
    bsz, seq, d = x.shape
    mod_spec = pl.BlockSpec((1, 1, d), lambda b, l: (b, 0, 0))
    ins = [x, a, sh, w.astype(BF16), gq, gk]
    specs = [_row_spec(tm, d), mod_spec, mod_spec, _const_spec(w.shape), _const_spec(gq.shape), _const_spec(gk.shape)]
    if rope is not None:
        ins += list(rope)
        specs += [pl.BlockSpec((tm, LANES), lambda b, l: (l, 0))] * 3
    return pl.pallas_call(
        functools.partial(_attn_qkv_kernel, use_rope=rope is not None, nq=nq, nk=nk),
        out_shape=(jax.ShapeDtypeStruct((bsz, seq, nq), BF16),
                   jax.ShapeDtypeStruct((bsz, seq, nk), kv_dtype),
                   jax.ShapeDtypeStruct((bsz, seq, nv), kv_dtype)),
        grid=(bsz, seq // tm),
        in_specs=specs,
        out_specs=(_row_spec(tm, nq), _row_spec(tm, nk), _row_spec(tm, nv)),
        compiler_params=_cparams("parallel", "parallel"),
        name="qkv",
    )(*ins)


def _sink_softmax_pv(s, sink, vh):
    m = jnp.maximum(jnp.max(s, axis=1, keepdims=True), sink)
    e = jnp.exp(s - m)
    den = jnp.sum(e, axis=1, keepdims=True) + jnp.exp(sink - m)
    p = e / den
    return _dot(p.astype(BF16), vh)


def _attn_ctx_kernel(sink_ref, q_ref, k_ref, v_ref, o_ref, *, kvh, group, hd):
    for g in range(kvh):
        kh = k_ref[0, :, g * hd:(g + 1) * hd].astype(BF16)
        vh = v_ref[0, :, g * hd:(g + 1) * hd].astype(BF16)
        for r in range(group):
            h = g * group + r
            s = _dot_nt(q_ref[0, :, h * hd:(h + 1) * hd], kh)
            o_ref[0, :, h * hd:(h + 1) * hd] = _sink_softmax_pv(s, sink_ref[h], vh).astype(o_ref.dtype)


def _attn_ctx(q, k, v, sink, kvh, group, hd):
    bsz, seq, nq = q.shape
    nk = k.shape[-1]
    return pl.pallas_call(
        functools.partial(_attn_ctx_kernel, kvh=kvh, group=group, hd=hd),
        out_shape=jax.ShapeDtypeStruct((bsz, seq, nq), BF16),
        grid=(bsz,),
        in_specs=[pl.BlockSpec(memory_space=pltpu.SMEM),
                  pl.BlockSpec((1, seq, nq), lambda b: (b, 0, 0)),
                  pl.BlockSpec((1, seq, nk), lambda b: (b, 0, 0)),
                  pl.BlockSpec((1, seq, nk), lambda b: (b, 0, 0))],
        out_specs=pl.BlockSpec((1, seq, nq), lambda b: (b, 0, 0)),
        compiler_params=_cparams("parallel"),
        name="attn_ctx",
    )(sink, q, k, v)


def _attn_lat_kernel(sink_ref, q_ref, kp_ref, ko_ref, kn_ref, vp_ref, vo_ref, vn_ref, ck_ref, cv_ref, o_ref, *,
                     kvh, group, hd, blk, nctx):
    n = pl.program_id(1)
    nb = pl.num_programs(1)
    nkeys = 3 * blk + nctx
    qi = lax.broadcasted_iota(jnp.int32, (blk, nkeys), 0)
    kj = lax.broadcasted_iota(jnp.int32, (blk, nkeys), 1)
    ok = (kj >= qi) & (kj <= qi + 2 * blk)
    ok = ok & ((kj >= blk) | (n > 0)) & ((kj < 2 * blk) | (n < nb - 1))
    ok = ok | (kj >= 3 * blk)
    bias = jnp.where(ok, 0.0, NEG_INF)
    for g in range(kvh):
        sl = slice(g * hd, (g + 1) * hd)
        kcat = jnp.concatenate([kp_ref[0, :, sl], ko_ref[0, :, sl], kn_ref[0, :, sl], ck_ref[0, :, sl].astype(BF16)],
                               axis=0)
        vcat = jnp.concatenate([vp_ref[0, :, sl], vo_ref[0, :, sl], vn_ref[0, :, sl], cv_ref[0, :, sl].astype(BF16)],
                               axis=0)
        for r in range(group):
            h = g * group + r
            s = _dot_nt(q_ref[0, :, h * hd:(h + 1) * hd], kcat) + bias
            o_ref[0, :, h * hd:(h + 1) * hd] = _sink_softmax_pv(s, sink_ref[h], vcat).astype(o_ref.dtype)


def _attn_lat(q, k, v, ck, cv, sink, kvh, group, hd, blk):
    bsz, seq, nq = q.shape
    nk = k.shape[-1]
    nctx = ck.shape[1]
    nb = seq // blk
    prev = pl.BlockSpec((1, blk, nk), lambda b, n: (b, jnp.maximum(n - 1, 0), 0))
    own = pl.BlockSpec((1, blk, nk), lambda b, n: (b, n, 0))
    nxt = pl.BlockSpec((1, blk, nk), lambda b, n: (b, jnp.minimum(n + 1, nb - 1), 0))
    ctx = pl.BlockSpec((1, nctx, nk), lambda b, n: (b, 0, 0))
    return pl.pallas_call(
        functools.partial(_attn_lat_kernel, kvh=kvh, group=group, hd=hd, blk=blk, nctx=nctx),
        out_shape=jax.ShapeDtypeStruct((bsz, seq, nq), BF16),
        grid=(bsz, nb),
        in_specs=[pl.BlockSpec(memory_space=pltpu.SMEM), pl.BlockSpec((1, blk, nq), lambda b, n: (b, n, 0)),
                  prev, own, nxt, prev, own, nxt, ctx, ctx],
        out_specs=pl.BlockSpec((1, blk, nq), lambda b, n: (b, n, 0)),
        compiler_params=_cparams("parallel", "parallel"),
        name="attn_lat",
    )(sink, q, k, k, k, v, v, v, ck, cv)


def _ssd_proj_kernel(x_ref, xp_ref, xn_ref, a_ref, sh_ref, wz_ref, wx_ref, wdt_ref, cw_ref, cb_ref,
                     z_ref, xs_ref, bm_ref, cm_ref, dt_ref, xe_ref, *, tm, nl, cn, inner, gs):
    xe_ref[...] = _ext_rows(x_ref, xp_ref, xn_ref, a_ref[0], sh_ref[0], nl).astype(BF16)
    xt = xe_ref[HALO:HALO + tm, :]
    z_ref[0] = _dot(xt, wz_ref[...])
    dt_ref[0] = _dot(xt, wdt_ref[...])
    for j in range((inner + 2 * gs) // cn):
        sl = slice(j * cn, (j + 1) * cn)
        h = _dot(xe_ref[...], wx_ref[:, sl])
        y = _silu(_dwconv_rows(h, cw_ref[:, sl], cb_ref[:, sl], 5))[HALO:HALO + tm]
        if j * cn < inner:
            xs_ref[0, :, sl] = y
        elif j * cn < inner + gs:
            bm_ref[0, :, j * cn - inner:(j + 1) * cn - inner] = y
        else:
            cm_ref[0, :, j * cn - inner - gs:(j + 1) * cn - inner - gs] = y


def _ssd_project(x, a, sh, w_in, conv_w, conv_b, inner, gs, ndt, tm):
    bsz, seq, d = x.shape
    nl = seq // tm
    cn = 512
    wz = w_in[:, :inner].astype(BF16)
    wx = w_in[:, inner:2 * inner + 2 * gs].astype(BF16)
    wdt = w_in[:, 2 * inner + 2 * gs:].astype(BF16)
    mod_spec = pl.BlockSpec((1, 1, d), lambda b, l: (b, 0, 0))
    sds = lambda n: jax.ShapeDtypeStruct((bsz, seq, n), F32)
    return pl.pallas_call(
        functools.partial(_ssd_proj_kernel, tm=tm, nl=nl, cn=cn, inner=inner, gs=gs),
        out_shape=(sds(inner), sds(inner), sds(gs), sds(gs), sds(ndt)),
        grid=(bsz, nl),
        in_specs=_halo_specs(tm, d, nl) + [mod_spec, mod_spec, _const_spec(wz.shape), _const_spec(wx.shape),
                                           _const_spec(wdt.shape), _const_spec(conv_w.shape),
                                           _const_spec((1, conv_b.shape[0]))],
        out_specs=(_row_spec(tm, inner), _row_spec(tm, inner), _row_spec(tm, gs), _row_spec(tm, gs),
                   _row_spec(tm, ndt)),
        scratch_shapes=[pltpu.VMEM((tm + 2 * HALO, d), BF16)],
        compiler_params=_cparams("parallel", "parallel"),
        name="ssd_proj",
    )(x, x, x, a, sh, wz, wx, wdt, conv_w, conv_b[None])


def _tri_masks(reverse):
    i = lax.broadcasted_iota(jnp.int32, (CHUNK, CHUNK), 0)
    j = lax.broadcasted_iota(jnp.int32, (CHUNK, CHUNK), 1)
    tri = (j >= i) if reverse else (j <= i)
    tri_t = (j <= i) if reverse else (j >= i)
    return tri, jnp.where(tri, 1.0, 0.0).astype(F32), jnp.where(tri_t, 1.0, 0.0).astype(F32)


def _ssd_chunk(x, bmat, cmat, dtc_raw, dtr_raw, bias_c, bias_r, a_c, a_r, state_ref, reverse, hp):
    nr = dtc_raw.shape[1]
    tri, t_mat, t_mat_t = _tri_masks(reverse)
    dt_c = _softplus(dtc_raw + bias_c)
    dt_r = _softplus(dtr_raw + bias_r)
    acs_c = _dot_exact(t_mat, dt_c * a_c)
    acs_r = _dot_exact(dt_r * a_r, t_mat_t)
    end = 0 if reverse else CHUNK - 1
    last = acs_c[end:end + 1]
    er = lax.broadcasted_iota(jnp.int32, (nr, nr * hp), 0)
    ec = lax.broadcasted_iota(jnp.int32, (nr, nr * hp), 1) // hp
    expand = jnp.where(er == ec, 1.0, 0.0).astype(F32)
    xdt = x * _dot_exact(dt_c, expand)
    decay_in = _dot_exact(jnp.exp(acs_c), expand)
    w_state = _dot_exact(jnp.exp(last - acs_c), expand)
    c_bf = cmat.astype(BF16)
    cb = _dot_nt(c_bf, bmat.astype(BF16))
    state = state_ref[...]
    y_inter = _dot(c_bf, state.astype(BF16)) * decay_in
    xdt_bf = xdt.astype(BF16)
    ys = []
    for r in range(nr):
        seg = acs_c[:, r:r + 1] - acs_r[r:r + 1, :]
        m = cb * jnp.exp(jnp.where(tri, seg, NEG_INF))
        ys.append(_dot(m.astype(BF16), xdt_bf[:, r * hp:(r + 1) * hp]))
    state_ref[...] = state * decay_in[end:end + 1] + _dot(bmat.T.astype(BF16), (xdt * w_state).astype(BF16))
    return jnp.concatenate(ys, axis=1) + y_inter


def _ssd_scan_kernel(xf_ref, bf_ref, cf_ref, dcf_ref, drf_ref, xb_ref, bb_ref, cb_ref, dcb_ref, drb_ref,
                     biasc_ref, biasr_ref, ac_ref, ar_ref, h0_ref, yf_ref, yb_ref, hout_ref, state_ref, *, hp):
    c = pl.program_id(2)

    @pl.when(c == 0)
    def _():
        state_ref[...] = h0_ref[0, :, 0]

    yf_ref[0] = _ssd_chunk(xf_ref[0], bf_ref[0], cf_ref[0], dcf_ref[0, 0, 0], drf_ref[0, 0, 0], biasc_ref[0, 0],
                           biasr_ref[0, 0], ac_ref[0, 0], ar_ref[0, 0], state_ref.at[0], False, hp)
    yb_ref[0] = _ssd_chunk(xb_ref[0], bb_ref[0], cb_ref[0], dcb_ref[0, 0, 0], drb_ref[0, 0, 0], biasc_ref[1, 0],
                           biasr_ref[1, 0], ac_ref[1, 0], ar_ref[1, 0], state_ref.at[1], True, hp)

    @pl.when(c == pl.num_programs(2) - 1)
    def _():
        hout_ref[0, :, 0] = state_ref[...]


def _ssd_scan(xs, bm, cm, dt_raw, dt_bias, a_log, h0, groups, hp):
    bsz, seq, inner = xs.shape
    heads = inner // hp
    nr = heads // groups
    ns = bm.shape[-1] // groups
    nc = seq // CHUNK
    gw = nr * hp
    dt5 = dt_raw.reshape(bsz, seq, 2, groups, nr)
    dt_col = dt5.transpose(0, 2, 3, 1, 4)
    dt_row = dt5.transpose(0, 2, 3, 4, 1)
    bias = dt_bias.astype(F32).reshape(2, groups, nr)
    a_neg = (-jnp.exp(a_log.astype(F32))).reshape(2, groups, nr)
    h0t = h0.reshape(bsz, 2, groups, nr, hp, ns).transpose(0, 1, 2, 5, 3, 4).reshape(bsz, 2, groups, ns, gw)

    def fwd(n, w):
        return pl.BlockSpec((1, CHUNK, w), lambda b, g, c: (b, c, g))

    def bwd(n, w):
        return pl.BlockSpec((1, CHUNK, w), lambda b, g, c: (b, nc - 1 - c, g))

    vec_c = pl.BlockSpec((2, 1, 1, nr), lambda b, g, c: (0, g, 0, 0))
    vec_r = pl.BlockSpec((2, 1, nr, 1), lambda b, g, c: (0, g, 0, 0))
    state_spec = pl.BlockSpec((1, 2, 1, ns, gw), lambda b, g, c: (b, 0, g, 0, 0))
    yf, yb, hout = pl.pallas_call(
        functools.partial(_ssd_scan_kernel, hp=hp),
        out_shape=(jax.ShapeDtypeStruct(xs.shape, F32), jax.ShapeDtypeStruct(xs.shape, F32),
                   jax.ShapeDtypeStruct(h0t.shape, F32)),
        grid=(bsz, groups, nc),
        in_specs=[fwd(seq, gw), fwd(seq, ns), fwd(seq, ns),
                  pl.BlockSpec((1, 1, 1, CHUNK, nr), lambda b, g, c: (b, 0, g, c, 0)),
                  pl.BlockSpec((1, 1, 1, nr, CHUNK), lambda b, g, c: (b, 0, g, 0, c)),
                  bwd(seq, gw), bwd(seq, ns), bwd(seq, ns),
                  pl.BlockSpec((1, 1, 1, CHUNK, nr), lambda b, g, c: (b, 1, g, nc - 1 - c, 0)),
                  pl.BlockSpec((1, 1, 1, nr, CHUNK), lambda b, g, c: (b, 1, g, 0, nc - 1 - c)),
                  vec_c, vec_r, vec_c, vec_r, state_spec],
        out_specs=(fwd(seq, gw), bwd(seq, gw), state_spec),
        scratch_shapes=[pltpu.VMEM((2, ns, gw), F32)],
        compiler_params=_cparams("parallel", "parallel", "arbitrary"),
        name="ssd_scan",
    )(xs, bm, cm, dt_col, dt_row, xs, bm, cm, dt_col, dt_row,
      bias[:, :, None, :], bias[:, :, :, None], a_neg[:, :, None, :], a_neg[:, :, :, None], h0t)
    hout = hout.reshape(bsz, 2, groups, ns, nr, hp).transpose(0, 1, 2, 4, 5, 3).reshape(bsz, 2, heads, hp, ns)
    return yf, yb, hout


def _ssd_out_prologue(yf_ref, yb_ref, xs_ref, z_ref, d_ref, g_ref):
    y = yf_ref[0] + yb_ref[0] + xs_ref[0] * d_ref[...]
    y = y * _silu(z_ref[0])
    ms = jnp.mean(y * y, axis=-1, keepdims=True)
    return y * lax.rsqrt(ms + EPS) * g_ref[...]


def _diff_attn_kernel(q_ref, k_ref, v_ref, lam_ref, gsub_ref, o_ref, m_ref, l_ref, acc_ref, *, hd):
    kk = pl.program_id(3)

    @pl.when(kk == 0)
    def _():
        m_ref[...] = jnp.full_like(m_ref, NEG_INF)
        l_ref[...] = jnp.zeros_like(l_ref)
        acc_ref[...] = jnp.zeros_like(acc_ref)

    q = q_ref[0]
    k = k_ref[0]
    v = v_ref[0]
    for c in range(2):
        s = _dot_nt(q[:, c * hd:(c + 1) * hd], k[:, c * hd:(c + 1) * hd])
        m_old = m_ref[c]
        m_new = jnp.maximum(m_old, jnp.max(s, axis=1, keepdims=True))
        alpha = jnp.exp(m_old - m_new)
        p = jnp.exp(s - m_new)
        l_ref[c] = alpha * l_ref[c] + jnp.sum(p, axis=1, keepdims=True)
        acc_ref[c] = alpha * acc_ref[c] + _dot(p.astype(BF16), v)
        m_ref[c] = m_new

    @pl.when(kk == pl.num_programs(3) - 1)
    def _():
        lv = lam_ref[...]
        f1 = jnp.exp(jnp.sum(lv[0:1] * lv[1:2], axis=1, keepdims=True))
        f2 = jnp.exp(jnp.sum(lv[2:3] * lv[3:4], axis=1, keepdims=True))
        lam = f1 - f2 + DIFF_LAMBDA_INIT
        o = acc_ref[0] / l_ref[0] - lam * (acc_ref[1] / l_ref[1])
        ms = jnp.mean(o * o, axis=-1, keepdims=True)
        o_ref[0] = (o * lax.rsqrt(ms + EPS) * gsub_ref[...] * (1.0 - DIFF_LAMBDA_INIT)).astype(o_ref.dtype)


def _diff_attn(q, k, v, lam_vecs, g_sub, heads, hd, tq, tk):
    bsz, lq, n = q.shape
    lk = k.shape[1]
    vd = 2 * hd
    return pl.pallas_call(
        functools.partial(_diff_attn_kernel, hd=hd),
        out_shape=jax.ShapeDtypeStruct((bsz, lq, n), BF16),
        grid=(bsz, heads, lq // tq, lk // tk),
        in_specs=[pl.BlockSpec((1, tq, vd), lambda b, h, i, j: (b, i, h)),
                  pl.BlockSpec((1, tk, vd), lambda b, h, i, j: (b, j, h)),
                  pl.BlockSpec((1, tk, vd), lambda b, h, i, j: (b, j, h)),
                  pl.BlockSpec(lam_vecs.shape, lambda b, h, i, j: (0, 0)),
                  pl.BlockSpec((1, vd), lambda b, h, i, j: (0, 0))],
        out_specs=pl.BlockSpec((1, tq, vd), lambda b, h, i, j: (b, i, h)),
        scratch_shapes=[pltpu.VMEM((2, tq, 1), F32), pltpu.VMEM((2, tq, 1), F32), pltpu.VMEM((2, tq, vd), F32)],
        compiler_params=_cparams("parallel", "parallel", "parallel", "arbitrary"),
        name="diff_attn",
    )(q, k, v, lam_vecs, g_sub[None])


def _ml_proj_kernel(x_ref, xp_ref, xn_ref, a_ref, sh_ref, wm_ref, wo_ref, cw_ref, cb_ref, wq_ref, wk_ref, wv_ref,
                    wif_ref, bif_ref, q_ref, k_ref, v_ref, g_ref, op_ref, xe_ref, *, tm, nl, heads, ihd, dk):
    xe_ref[...] = _ext_rows(x_ref, xp_ref, xn_ref, a_ref[0], sh_ref[0], nl).astype(BF16)
    op_ref[0] = _dot(xe_ref[HALO:HALO + tm, :], wo_ref[...])
    gates = jnp.zeros((tm, bif_ref.shape[1]), F32) + bif_ref[...]
    for h in range(heads):
        sl = slice(h * ihd, (h + 1) * ihd)
        xm_e = _dot(xe_ref[...], wm_ref[:, sl])
        xc = _silu(_dwconv_rows(xm_e, cw_ref[:, sl], cb_ref[:, sl], 5))[HALO:HALO + tm].astype(BF16)
        xm = xm_e[HALO:HALO + tm].astype(BF16)
        q_ref[0, :, h * dk:(h + 1) * dk] = _dot(xc, wq_ref[h]).astype(q_ref.dtype)
        k_ref[0, :, h * dk:(h + 1) * dk] = (_dot(xc, wk_ref[h]) * (dk ** -0.5)).astype(k_ref.dtype)
        v_ref[0, :, sl] = _dot(xm, wv_ref[h]).astype(v_ref.dtype)
        gates = gates + _dot(xc, wif_ref[sl, :])
    g_ref[0] = gates


def _ml_project(x, a, sh, w_in, conv_w, conv_b, w_q, w_k, w_v, w_if, b_if, tm):
    bsz, seq, d = x.shape
    heads, ihd, dk = w_q.shape
    inner = heads * ihd
    nl = seq // tm
    ng = w_if.shape[1]
    wm = w_in[:, :inner].astype(BF16)
    wo = w_in[:, inner:].astype(BF16)
    mod_spec = pl.BlockSpec((1, 1, d), lambda b, l: (b, 0, 0))
    sds = lambda n, dt: jax.ShapeDtypeStruct((bsz, seq, n), dt)
    return pl.pallas_call(
        functools.partial(_ml_proj_kernel, tm=tm, nl=nl, heads=heads, ihd=ihd, dk=dk),
        out_shape=(sds(heads * dk, BF16), sds(heads * dk, BF16), sds(inner, BF16), sds(ng, F32), sds(inner, F32)),
        grid=(bsz, nl),
        in_specs=_halo_specs(tm, d, nl) + [mod_spec, mod_spec, _const_spec(wm.shape), _const_spec(wo.shape),
                                           _const_spec(conv_w.shape), _const_spec((1, inner)),
                                           _const_spec(w_q.shape), _const_spec(w_k.shape), _const_spec(w_v.shape),
                                           _const_spec(w_if.shape), _const_spec((1, ng))],
        out_specs=(_row_spec(tm, heads * dk), _row_spec(tm, heads * dk), _row_spec(tm, inner), _row_spec(tm, ng),
                   _row_spec(tm, inner)),
        scratch_shapes=[pltpu.VMEM((tm + 2 * HALO, d), BF16)],
        compiler_params=_cparams("parallel", "parallel"),
        name="ml_proj",
    )(x, x, x, a, sh, wm, wo, conv_w, conv_b[None], w_q.astype(BF16), w_k.astype(BF16), w_v.astype(BF16),
      w_if.astype(BF16), b_if.reshape(1, ng))


def _ml_chunk(q, k, v, gc, gr, d, c_ref, n_ref, m_ref, reverse):
    tri, t_mat, t_mat_t = _tri_masks(reverse)
    bcum_c = _dot_exact(t_mat, _log_sigmoid(gc))[:, 2 * d + 1:2 * d + 2]
    bcum_r = _dot_exact(_log_sigmoid(gr), t_mat_t)[2 * d + 1:2 * d + 2, :]
    i_c = gc[:, 2 * d:2 * d + 1]
    i_r = gr[2 * d:2 * d + 1, :]
    m_old = m_ref[...]
    cmat = c_ref[...]
    nrow = n_ref[...]
    dmat = jnp.where(tri, bcum_c - bcum_r + i_r, NEG_INF)
    inter = bcum_c + m_old
    mt = jnp.maximum(jnp.max(dmat, axis=1, keepdims=True), inter)
    s = _dot_nt(q, k) * jnp.exp(dmat - mt)
    w_int = jnp.exp(inter - mt)
    num = _dot(s.astype(BF16), v) + w_int * _dot(q, cmat.astype(BF16))
    den = jnp.sum(s, axis=1, keepdims=True) + w_int * jnp.sum(q.astype(F32) * nrow, axis=1, keepdims=True)
    h = num / jnp.maximum(jnp.abs(den), jnp.exp(-mt))
    end = 0 if reverse else CHUNK - 1
    btot = bcum_c[end:end + 1]
    g = btot - bcum_c + i_c
    m_new = jnp.maximum(btot + m_old, jnp.max(g, axis=0, keepdims=True))
    decay = jnp.exp(btot + m_old - m_new)
    kw = k.astype(F32) * jnp.exp(g - m_new)
    c_ref[...] = cmat * decay + _dot(kw.T.astype(BF16), v)
    n_ref[...] = nrow * decay + jnp.sum(kw, axis=0, keepdims=True)
    m_ref[...] = m_new
    return h


def _ml_scan_kernel(qf_ref, kf_ref, vf_ref, gcf_ref, grf_ref, qb_ref, kb_ref, vb_ref, gcb_ref, grb_ref,
                    c0_ref, n0_ref, m0_ref, hf_ref, hb_ref, cout_ref, nout_ref, mout_ref, c_ref, n_ref, m_ref):
    c = pl.program_id(2)

    @pl.when(c == 0)
    def _():
        c_ref[...] = c0_ref[0, :, 0]
        n_ref[...] = n0_ref[0, :, 0]
        m_ref[...] = m0_ref[0, :, 0]

    hf_ref[0] = _ml_chunk(qf_ref[0], kf_ref[0], vf_ref[0], gcf_ref[0, 0], grf_ref[0, 0], 0,
                          c_ref.at[0], n_ref.at[0], m_ref.at[0], False)
    hb_ref[0] = _ml_chunk(qb_ref[0], kb_ref[0], vb_ref[0], gcb_ref[0, 0], grb_ref[0, 0], 1,
                          c_ref.at[1], n_ref.at[1], m_ref.at[1], True)

    @pl.when(c == pl.num_programs(2) - 1)
    def _():
        cout_ref[0, :, 0] = c_ref[...]
        nout_ref[0, :, 0] = n_ref[...]
        mout_ref[0, :, 0] = m_ref[...]


def _ml_scan(q, k, v, gates, c0, n0, m0):
    bsz, seq, _ = q.shape
    _, _, heads, dk, dv = c0.shape
    nc = seq // CHUNK
    g4 = gates.reshape(bsz, seq, 4, heads)
    g8 = jnp.concatenate([g4, jnp.zeros_like(g4)], axis=2)
    g_col = g8.transpose(0, 3, 1, 2)
    g_row = g8.transpose(0, 3, 2, 1)
    n0 = n0.reshape(bsz, 2, heads, 1, dk)
    m0 = m0.reshape(bsz, 2, heads, 1, 1)

    def fwd(w):
        return pl.BlockSpec((1, CHUNK, w), lambda b, h, c: (b, c, h))

    def bwd(w):
        return pl.BlockSpec((1, CHUNK, w), lambda b, h, c: (b, nc - 1 - c, h))

    c_spec = pl.BlockSpec((1, 2, 1, dk, dv), lambda b, h, c: (b, 0, h, 0, 0))
    n_spec = pl.BlockSpec((1, 2, 1, 1, dk), lambda b, h, c: (b, 0, h, 0, 0))
    m_spec = pl.BlockSpec((1, 2, 1, 1, 1), lambda b, h, c: (b, 0, h, 0, 0))
    hf, hb, cout, nout, mout = pl.pallas_call(
        _ml_scan_kernel,
        out_shape=(jax.ShapeDtypeStruct(v.shape, F32), jax.ShapeDtypeStruct(v.shape, F32),
                   jax.ShapeDtypeStruct(c0.shape, F32), jax.ShapeDtypeStruct(n0.shape, F32),
                   jax.ShapeDtypeStruct(m0.shape, F32)),
        grid=(bsz, heads, nc),
        in_specs=[fwd(dk), fwd(dk), fwd(dv),
                  pl.BlockSpec((1, 1, CHUNK, 8), lambda b, h, c: (b, h, c, 0)),
                  pl.BlockSpec((1, 1, 8, CHUNK), lambda b, h, c: (b, h, 0, c)),
                  bwd(dk), bwd(dk), bwd(dv),
                  pl.BlockSpec((1, 1, CHUNK, 8), lambda b, h, c: (b, h, nc - 1 - c, 0)),
                  pl.BlockSpec((1, 1, 8, CHUNK), lambda b, h, c: (b, h, 0, nc - 1 - c)),
                  c_spec, n_spec, m_spec],
        out_specs=(fwd(dv), bwd(dv), c_spec, n_spec, m_spec),
        scratch_shapes=[pltpu.VMEM((2, dk, dv), F32), pltpu.VMEM((2, 1, dk), F32), pltpu.VMEM((2, 1, 1), F32)],
        compiler_params=_cparams("parallel", "parallel", "arbitrary"),
        name="ml_scan",
    )(q, k, v, g_col, g_row, q, k, v, g_col, g_row, c0, n0, m0)
    return hf, hb, cout, nout.reshape(bsz, 2, heads, dk), mout.reshape(bsz, 2, heads)


def _ml_out_prologue(hf_ref, hb_ref, op_ref, g_ref, *, heads, dv):
    h = hf_ref[0] + hb_ref[0]
    outs = []
    for hh in range(heads):
        blk = h[:, hh * dv:(hh + 1) * dv]
        ms = jnp.mean(blk * blk, axis=-1, keepdims=True)
        outs.append(blk * lax.rsqrt(ms + EPS))
    return jax.nn.sigmoid(op_ref[0]) * (jnp.concatenate(outs, axis=1) * g_ref[...])


def _stream_mods(mod, g1, g2):
    a1 = (g1 * (1.0 + mod[:, 1]))[:, None]
    a2 = (g2 * (1.0 + mod[:, 4]))[:, None]
    return (a1, mod[:, 0][:, None], mod[:, 2][:, None]), (a2, mod[:, 3][:, None], mod[:, 5][:, None])


def _tile(seq):
    return min(seq, 512)


def _attn_layer(x, mods, lat_cache, w_qkv, g_q, g_k, sink, w_o):
    a, sh, gt = mods
    bsz, seq, d = x.shape
    hd = g_q.shape[0]
    heads = sink.shape[0]
    kvh = (w_qkv.shape[1] // hd - heads) // 2
    tm = _tile(seq)
    gq = jnp.tile(g_q, LANES // hd)[None]
    gk = jnp.tile(g_k, LANES // hd)[None]
    if lat_cache is None:
        q, k, v = _qkv_project(x, a, sh, w_qkv, gq, gk, heads * hd, kvh * hd, kvh * hd, None, F32, tm)
        o = _attn_ctx(q, k, v, sink, kvh, heads // kvh, hd)
    else:
        ck, cv = lat_cache
        q, k, v = _qkv_project(x, a, sh, w_qkv, gq, gk, heads * hd, kvh * hd, kvh * hd, _rope_tables(seq, hd), BF16, tm)
        o = _attn_lat(q, k, v, ck.reshape(bsz, -1, kvh * hd), cv.reshape(bsz, -1, kvh * hd), sink, kvh,
                      heads // kvh, hd, 128)
    x = _oproj(lambda o_ref: o_ref[0], [o], [_row_spec(tm, heads * hd)], x, gt, w_o, tm)
    return x, k.reshape(bsz, seq, kvh, hd), v.reshape(bsz, seq, kvh, hd)


def _ssd_layer(x, mods, h0, w_in, conv_w, conv_b, dt_bias, a_log, d_skip, g_norm, w_out):
    a, sh, gt = mods
    bsz, seq, d = x.shape
    heads = d_skip.shape[0]
    inner = g_norm.shape[0]
    hp = inner // heads
    gs = (conv_w.shape[1] - inner) // 2
    groups = gs // LANES
    tm = _tile(seq)
    z, xs, bm, cm, dt_raw = _ssd_project(x, a, sh, w_in, conv_w, conv_b, inner, gs, 2 * heads, tm)
    yf, yb, hout = _ssd_scan(xs, bm, cm, dt_raw, dt_bias, a_log, h0, groups, hp)
    rs = _row_spec(tm, inner)
    x = _oproj(_ssd_out_prologue, [yf, yb, xs, z, jnp.repeat(d_skip, hp)[None], g_norm[None]],
               [rs, rs, rs, rs, _const_spec((1, inner)), _const_spec((1, inner))], x, gt, w_out, tm)
    return x, hout


def _diff_layer(x, mods, lat_cache, w_qkv, g_q, g_k, lam_vecs, g_sub, w_o):
    a, sh, gt = mods
    bsz, seq, d = x.shape
    hd = g_q.shape[1]
    vd = g_sub.shape[0]
    heads = w_qkv.shape[1] // (4 * hd + vd)
    nq = heads * 2 * hd
    tm = _tile(seq)
    gq = g_q.reshape(1, 2 * hd)
    gk = g_k.reshape(1, 2 * hd)
    if lat_cache is None:
        q, k, v = _qkv_project(x, a, sh, w_qkv, gq, gk, nq, nq, heads * vd, None, F32, tm)
        o = _diff_attn(q, k.astype(BF16), v.astype(BF16), lam_vecs, g_sub, heads, hd, seq, seq)
    else:
        ck, cv = lat_cache
        q, k, v = _qkv_project(x, a, sh, w_qkv, gq, gk, nq, nq, heads * vd, _rope_tables(seq, hd), BF16, tm)
        kcat = jnp.concatenate([k, ck.reshape(bsz, -1, nq).astype(BF16)], axis=1)
        vcat = jnp.concatenate([v, cv.reshape(bsz, -1, heads * vd).astype(BF16)], axis=1)
        o = _diff_attn(q, kcat, vcat, lam_vecs, g_sub, heads, hd, 256, 512)
    x = _oproj(lambda o_ref: o_ref[0], [o], [_row_spec(tm, heads * vd)], x, gt, w_o, tm)
    return x, k.reshape(bsz, seq, heads, 2, hd), v.reshape(bsz, seq, heads, vd)


def _ml_layer(x, mods, c0, n0, m0, w_in, conv_w, conv_b, w_q, w_k, w_v, w_if, b_if, g_norm, w_out):
    a, sh, gt = mods
    bsz, seq, d = x.shape
    heads, dv = g_norm.shape
    inner = heads * dv
    tm = _tile(seq)
    q, k, v, gates, o_pre = _ml_project(x, a, sh, w_in, conv_w, conv_b, w_q, w_k, w_v, w_if, b_if, tm)
    hf, hb, cout, nout, mout = _ml_scan(q, k, v, gates, c0, n0, m0)
    rs = _row_spec(tm, inner)
    x = _oproj(functools.partial(_ml_out_prologue, heads=heads, dv=dv), [hf, hb, o_pre, g_norm.reshape(1, inner)],
               [rs, rs, rs, _const_spec((1, inner))], x, gt, w_out, tm)
    return x, cout, nout, mout


def kernel(x_prompt, x_sample, cache_attn_k, cache_attn_v, state_ssd, cache_diff_k, cache_diff_v, state_mlstm_c, state_mlstm_n, state_mlstm_m, c, c_ctx, ada_w, ada_b, norm1_g, norm2_g, ffn_w_up, ffn_conv_w, ffn_conv_b, ffn_w_down, attn_w_qkv, attn_g_q, attn_g_k, attn_sink, attn_w_o, ssd_w_in, ssd_conv_w, ssd_conv_b, ssd_dt_bias, ssd_a_log, ssd_d, ssd_g_norm, ssd_w_out, diff_w_qkv, diff_g_q, diff_g_k, diff_lq1, diff_lk1, diff_lq2, diff_lk2, diff_g_sub, diff_w_o, ml_w_in, ml_conv_w, ml_conv_b, ml_w_q, ml_w_k, ml_w_v, ml_w_if, ml_b_if, ml_g_norm, ml_w_out):
    xp, xs = x_prompt, x_sample
    bp, bs = xp.shape[0], xs.shape[0]
    d = xp.shape[-1]
    depth = ada_w.shape[0]
    rows = 16
    cond = jnp.concatenate([c_ctx[None], c, jnp.zeros((rows - 1 - bs, d), F32)], axis=0)
    mod = _ada(cond, ada_w, ada_b).reshape(depth, rows, 6, d)
    lam_vecs = jnp.stack([diff_lq1, diff_lk1, diff_lq2, diff_lk2])
    outs = {}
    for i in range(depth):
        mp1, mp2 = _stream_mods(jnp.broadcast_to(mod[i, 0], (bp, 6, d)), norm1_g[i], norm2_g[i])
        ms1, ms2 = _stream_mods(mod[i, 1:1 + bs], norm1_g[i], norm2_g[i])
        kind = i % 4
        if kind == 0:
            xp, outs["ak"], outs["av"] = _attn_layer(xp, mp1, None, attn_w_qkv, attn_g_q, attn_g_k, attn_sink, attn_w_o)
            xs, _, _ = _attn_layer(xs, ms1, (cache_attn_k, cache_attn_v), attn_w_qkv, attn_g_q, attn_g_k, attn_sink,
                                   attn_w_o)
        elif kind == 1:
            ssd_w = (ssd_w_in, ssd_conv_w, ssd_conv_b, ssd_dt_bias, ssd_a_log, ssd_d, ssd_g_norm, ssd_w_out)
            xp, outs["ssd"] = _ssd_layer(xp, mp1, jnp.zeros((bp,) + state_ssd.shape[1:], F32), *ssd_w)
            xs, _ = _ssd_layer(xs, ms1, state_ssd, *ssd_w)
        elif kind == 2:
            diff_w = (diff_w_qkv, diff_g_q, diff_g_k, lam_vecs, diff_g_sub, diff_w_o)
            xp, outs["dk"], outs["dv"] = _diff_layer(xp, mp1, None, *diff_w)
            xs, _, _ = _diff_layer(xs, ms1, (cache_diff_k, cache_diff_v), *diff_w)
        else:
            ml_w = (ml_w_in, ml_conv_w, ml_conv_b, ml_w_q, ml_w_k, ml_w_v, ml_w_if, ml_b_if, ml_g_norm, ml_w_out)
            zc = jnp.zeros((bp,) + state_mlstm_c.shape[1:], F32)
            zn = jnp.zeros((bp,) + state_mlstm_n.shape[1:], F32)
            zm = jnp.zeros((bp,) + state_mlstm_m.shape[1:], F32)
            xp, outs["mc"], outs["mn"], outs["mm"] = _ml_layer(xp, mp1, zc, zn, zm, *ml_w)
            xs, _, _, _ = _ml_layer(xs, ms1, state_mlstm_c, state_mlstm_n, state_mlstm_m, *ml_w)
        ffn_w = (ffn_w_up[i], ffn_conv_w[i], ffn_conv_b[i], ffn_w_down[i])
        xp = _ffn(xp, *mp2, *ffn_w, tm=_tile(xp.shape[1]))
        xs = _ffn(xs, *ms2, *ffn_w, tm=_tile(xs.shape[1]))
    return (xp, xs, outs["ak"], outs["av"], outs["ssd"], outs["dk"], outs["dv"], outs["mc"], outs["mn"], outs["mm"])
```

```python
import functools
import math

import jax
import jax.numpy as jnp
from jax import lax
from jax.experimental import pallas as pl
from jax.experimental.pallas import tpu as pltpu

F32 = jnp.float32
BF16 = jnp.bfloat16
HIGHEST = lax.Precision.HIGHEST

EPS = 1e-6
ROPE_THETA = 10000.0
GRID_W = 64
DIFF_LAMBDA_INIT = 0.8 - 0.6 * math.exp(-0.3 * 2)

VMEM_LIMIT_BYTES = 56 * 1024 * 1024
LANES = 128
HALO = 16
CHUNK = 128
NEG_INF = float("-inf")
LOG2E = math.log2(math.e)


def _cparams(*sem):
    return pltpu.CompilerParams(dimension_semantics=sem, vmem_limit_bytes=VMEM_LIMIT_BYTES)


def _dot(a, b):
    return jnp.dot(a, b, preferred_element_type=F32)


def _dot_nt(a, b):
    return lax.dot_general(a, b, (((1,), (1,)), ((), ())), preferred_element_type=F32)


def _dot_exact(a, b):
    return jnp.dot(a, b, preferred_element_type=F32, precision=HIGHEST)


def _norm_mod(x, a, sh):
    ms = jnp.mean(x * x, axis=-1, keepdims=True)
    return x * lax.rsqrt(ms + EPS) * a + sh


def _silu(x):
    return x * jax.nn.sigmoid(x)


def _softplus(x):
    return jnp.maximum(x, 0.0) + jnp.log(1.0 + jnp.exp(-jnp.abs(x)))


def _log_sigmoid(x):
    return jnp.minimum(x, 0.0) - jnp.log(1.0 + jnp.exp(-jnp.abs(x)))


def _const_spec(shape):
    nd = len(shape)
    return pl.BlockSpec(shape, lambda *_: (0,) * nd)


def _ext_rows(x_ref, xp_ref, xn_ref, a, sh, nl):
    l = pl.program_id(1)
    xt = _norm_mod(x_ref[0], a, sh)
    xp = _norm_mod(xp_ref[0], a, sh) * (l > 0).astype(F32)
    xn = _norm_mod(xn_ref[0], a, sh) * (l < nl - 1).astype(F32)
    return jnp.concatenate([xp, xt, xn], axis=0)


def _halo_specs(tm, d, nl):
    r = tm // HALO
    return [
        pl.BlockSpec((1, tm, d), lambda b, l: (b, l, 0)),
        pl.BlockSpec((1, HALO, d), lambda b, l: (b, jnp.maximum(l * r - 1, 0), 0)),
        pl.BlockSpec((1, HALO, d), lambda b, l: (b, jnp.minimum((l + 1) * r, nl * r - 1), 0)),
    ]


def _dwconv_rows(h, w, b, width):
    rows = h.shape[0]
    pad = width // 2
    y = h * w[pad:pad + 1]
    for k in range(width):
        if k == pad:
            continue
        y = y + pltpu.roll(h, (pad - k) % rows, 0) * w[k:k + 1]
    return y + b


def _ada_kernel(c_ref, w_ref, b_ref, o_ref):
    s = _silu(c_ref[...]).astype(BF16)
    o_ref[0] = _dot(s, w_ref[0].astype(BF16)) + b_ref[0]


def _ada(cond, ada_w, ada_b):
    depth, d, n = ada_w.shape
    r = cond.shape[0]
    tn = 1536
    return pl.pallas_call(
        _ada_kernel,
        out_shape=jax.ShapeDtypeStruct((depth, r, n), F32),
        grid=(depth, n // tn),
        in_specs=[
            pl.BlockSpec((r, d), lambda i, j: (0, 0)),
            pl.BlockSpec((1, d, tn), lambda i, j: (i, 0, j)),
            pl.BlockSpec((1, 1, tn), lambda i, j: (i, 0, j)),
        ],
        out_specs=pl.BlockSpec((1, r, tn), lambda i, j: (i, 0, j)),
        compiler_params=_cparams("parallel", "parallel"),
        name="ada",
    )(cond, ada_w, ada_b.reshape(depth, 1, n))


def _ffn_kernel(x_ref, xp_ref, xn_ref, a_ref, sh_ref, gt_ref, wa_ref, wg_ref, cwa_ref, cwg_ref, cba_ref, cbg_ref,
                wd_ref, o_ref, xe_ref, acc_ref, *, tm, nl, nchunk):
    xe_ref[...] = _ext_rows(x_ref, xp_ref, xn_ref, a_ref[0], sh_ref[0], nl).astype(BF16)

    def up(j):
        xe = xe_ref[...]
        return _dot(xe, wa_ref[j]), _dot(xe, wg_ref[j])

    nxt = up(0)
    for j in range(nchunk):
        ua, ug = nxt
        if j + 1 < nchunk:
            nxt = up(j + 1)
        ha = _dwconv_rows(ua, cwa_ref[j], cba_ref[j], 3)[HALO:HALO + tm]
        hg = _dwconv_rows(ug, cwg_ref[j], cbg_ref[j], 3)[HALO:HALO + tm]
        part = _dot((ha * _silu(hg)).astype(BF16), wd_ref[j])
        if j == 0:
            acc_ref[...] = part
        else:
            acc_ref[...] += part
    o_ref[0] = x_ref[0] + gt_ref[0] * acc_ref[...]


def _ffn(x, a, sh, gt, w_up, conv_w, conv_b, w_down, tm):
    bsz, seq, d = x.shape
    dff = w_down.shape[0]
    cn = 256
    nchunk = dff // cn
    nl = seq // tm
    wa = w_up[:, :dff].reshape(d, nchunk, cn).transpose(1, 0, 2).astype(BF16)
    wg = w_up[:, dff:].reshape(d, nchunk, cn).transpose(1, 0, 2).astype(BF16)
    cwa = conv_w[:, :dff].reshape(3, nchunk, cn).transpose(1, 0, 2)
    cwg = conv_w[:, dff:].reshape(3, nchunk, cn).transpose(1, 0, 2)
    cba = conv_b[:dff].reshape(nchunk, 1, cn)
    cbg = conv_b[dff:].reshape(nchunk, 1, cn)
    wd = w_down.reshape(nchunk, cn, d).astype(BF16)
    mod_spec = pl.BlockSpec((1, 1, d), lambda b, l: (b, 0, 0))
    return pl.pallas_call(
        functools.partial(_ffn_kernel, tm=tm, nl=nl, nchunk=nchunk),
        out_shape=jax.ShapeDtypeStruct(x.shape, F32),
        grid=(bsz, nl),
        in_specs=_halo_specs(tm, d, nl) + [mod_spec, mod_spec, mod_spec,
                                       _const_spec(wa.shape), _const_spec(wg.shape), _const_spec(cwa.shape),
                                       _const_spec(cwg.shape), _const_spec(cba.shape), _const_spec(cbg.shape),
                                       _const_spec(wd.shape)],
        out_specs=pl.BlockSpec((1, tm, d), lambda b, l: (b, l, 0)),
        scratch_shapes=[pltpu.VMEM((tm + 2 * HALO, d), BF16), pltpu.VMEM((tm, d), F32)],
        compiler_params=_cparams("parallel", "parallel"),
        name="ffn",
    )(x, x, x, a, sh, gt, wa, wg, cwa, cwg, cba, cbg, wd)


def _oproj_kernel(*refs, prologue, n_in):
    in_refs = refs[:n_in]
    x_ref, gt_ref, w_ref, o_ref = refs[n_in:]
    lhs = prologue(*in_refs)
    o_ref[0] = x_ref[0] + gt_ref[0] * _dot(lhs.astype(BF16), w_ref[...])


def _oproj(prologue, ins, in_specs, x, gt, w, tm):
    bsz, seq, d = x.shape
    return pl.pallas_call(
        functools.partial(_oproj_kernel, prologue=prologue, n_in=len(ins)),
        out_shape=jax.ShapeDtypeStruct(x.shape, F32),
        grid=(bsz, seq // tm),
        in_specs=list(in_specs) + [pl.BlockSpec((1, tm, d), lambda b, l: (b, l, 0)),
                                   pl.BlockSpec((1, 1, d), lambda b, l: (b, 0, 0)),
                                   _const_spec(w.shape)],
        out_specs=pl.BlockSpec((1, tm, d), lambda b, l: (b, l, 0)),
        compiler_params=_cparams("parallel", "parallel"),
        name="oproj",
    )(*ins, x, gt, w.astype(BF16))


def _row_spec(tm, n):
    return pl.BlockSpec((1, tm, n), lambda b, l: (b, l, 0))


def _head_sumsq_matrix(hd):
    i = lax.broadcasted_iota(jnp.int32, (LANES, LANES), 0) // hd
    j = lax.broadcasted_iota(jnp.int32, (LANES, LANES), 1) // hd
    return jnp.where(i == j, 1.0 / hd, 0.0).astype(BF16)


def _qk_norm_rope(y, gain, rope, avg):
    outs = []
    for c in range(y.shape[1] // LANES):
        yb = y[:, c * LANES:(c + 1) * LANES]
        ms = _dot((yb * yb).astype(BF16), avg)
        yb = yb * lax.rsqrt(ms + EPS) * gain
        if rope is not None:
            cos, s_lo, s_hi = rope
            yb = yb * cos + pltpu.roll(yb, LANES - 32, 1) * s_lo + pltpu.roll(yb, 32, 1) * s_hi
        outs.append(yb)
    return outs


def _rope_tables(seq, hd):
    rows = seq // GRID_W
    row = jnp.repeat(jnp.arange(rows, dtype=F32), GRID_W)
    col = jnp.tile(jnp.arange(GRID_W, dtype=F32), rows)
    nf = hd // 4
    inv = ROPE_THETA ** (-jnp.arange(nf, dtype=F32) / nf)
    ang = jnp.concatenate([row[:, None] * inv, col[:, None] * inv], axis=-1)
    cos, sin = jnp.cos(ang), jnp.sin(ang)
    zero = jnp.zeros_like(sin)
    rep = LANES // hd
    cos_t = jnp.tile(jnp.concatenate([cos, cos], axis=-1), (1, rep))
    s_lo = jnp.tile(jnp.concatenate([-sin, zero], axis=-1), (1, rep))
    s_hi = jnp.tile(jnp.concatenate([zero, sin], axis=-1), (1, rep))
    return cos_t, s_lo, s_hi


def _attn_qkv_kernel(*refs, use_rope, nq, nk, qscale):
    if use_rope:
        x_ref, a_ref, sh_ref, w_ref, gq_ref, gk_ref, cos_ref, slo_ref, shi_ref, q_ref, k_ref, v_ref = refs
        rope = (cos_ref[...], slo_ref[...], shi_ref[...])
    else:
        x_ref, a_ref, sh_ref, w_ref, gq_ref, gk_ref, q_ref, k_ref, v_ref = refs
        rope = None
    xn = _norm_mod(x_ref[0], a_ref[0], sh_ref[0]).astype(BF16)
    y = _dot(xn, w_ref[...])
    avg = _head_sumsq_matrix(64)
    q = _qk_norm_rope(y[:, :nq], gq_ref[...], rope, avg)
    k = _qk_norm_rope(y[:, nq:nq + nk], gk_ref[...], rope, avg)
    for c, blk in enumerate(q):
        q_ref[0, :, c * LANES:(c + 1) * LANES] = (blk * qscale).astype(q_ref.dtype)
    for c, blk in enumerate(k):
        k_ref[0, :, c * LANES:(c + 1) * LANES] = blk.astype(k_ref.dtype)
    v_ref[0] = y[:, nq + nk:].astype(v_ref.dtype)


def _qkv_project(x, a, sh, w, gq, gk, nq, nk, nv, rope, kv_dtype, tm, qscale):
    bsz, seq, d = x.shape
    mod_spec = pl.BlockSpec((1, 1, d), lambda b, l: (b, 0, 0))
    ins = [x, a, sh, w.astype(BF16), gq, gk]
    specs = [_row_spec(tm, d), mod_spec, mod_spec, _const_spec(w.shape), _const_spec(gq.shape), _const_spec(gk.shape)]
    if rope is not None:
        ins += list(rope)
        specs += [pl.BlockSpec((tm, LANES), lambda b, l: (l, 0))] * 3
    return pl.pallas_call(
        functools.partial(_attn_qkv_kernel, use_rope=rope is not None, nq=nq, nk=nk, qscale=qscale),
        out_shape=(jax.ShapeDtypeStruct((bsz, seq, nq), BF16),
                   jax.ShapeDtypeStruct((bsz, seq, nk), kv_dtype),
                   jax.ShapeDtypeStruct((bsz, seq, nv), kv_dtype)),
        grid=(bsz, seq // tm),
        in_specs=specs,
        out_specs=(_row_spec(tm, nq), _row_spec(tm, nk), _row_spec(tm, nv)),
        compiler_params=_cparams("parallel", "parallel"),
        name="qkv",
    )(*ins)


def _sink_softmax_pv(parts, sink2):
    m = sink2
    for s, _ in parts:
        m = jnp.maximum(m, jnp.max(s, axis=1, keepdims=True))
    den = jnp.exp2(sink2 - m)
    acc = None
    for s, v in parts:
        e = jnp.exp2(s - m)
        den = den + jnp.sum(e, axis=1, keepdims=True)
        pv = _dot(e.astype(BF16), v)
        acc = pv if acc is None else acc + pv
    return acc / den


def _attn_ctx_kernel(sink_ref, q_ref, k_ref, v_ref, o_ref, *, kvh, group, hd):
    for g in range(kvh):
        kh = k_ref[0, :, g * hd:(g + 1) * hd].astype(BF16)
        vh = v_ref[0, :, g * hd:(g + 1) * hd].astype(BF16)
        for r in range(group):
            h = g * group + r
            s = _dot_nt(q_ref[0, :, h * hd:(h + 1) * hd], kh)
            o_ref[0, :, h * hd:(h + 1) * hd] = _sink_softmax_pv([(s, vh)], sink_ref[h] * LOG2E).astype(o_ref.dtype)


def _attn_ctx(q, k, v, sink, kvh, group, hd):
    bsz, seq, nq = q.shape
    nk = k.shape[-1]
    return pl.pallas_call(
        functools.partial(_attn_ctx_kernel, kvh=kvh, group=group, hd=hd),
        out_shape=jax.ShapeDtypeStruct((bsz, seq, nq), BF16),
        grid=(bsz,),
        in_specs=[pl.BlockSpec(memory_space=pltpu.SMEM),
                  pl.BlockSpec((1, seq, nq), lambda b: (b, 0, 0)),
                  pl.BlockSpec((1, seq, nk), lambda b: (b, 0, 0)),
                  pl.BlockSpec((1, seq, nk), lambda b: (b, 0, 0))],
        out_specs=pl.BlockSpec((1, seq, nq), lambda b: (b, 0, 0)),
        compiler_params=_cparams("parallel"),
        name="attn_ctx",
    )(sink, q, k, v)


def _attn_lat_kernel(sink_ref, q_ref, kp_ref, ko_ref, kn_ref, vp_ref, vo_ref, vn_ref, ck_ref, cv_ref, o_ref, *,
                     kvh, group, hd, blk):
    n = pl.program_id(1)
    nb = pl.num_programs(1)
    qi = lax.broadcasted_iota(jnp.int32, (blk, 3 * blk), 0)
    kj = lax.broadcasted_iota(jnp.int32, (blk, 3 * blk), 1)
    ok = (kj >= qi) & (kj <= qi + 2 * blk)
    ok = ok & ((kj >= blk) | (n > 0)) & ((kj < 2 * blk) | (n < nb - 1))
    bias = jnp.where(ok, 0.0, NEG_INF)
    for g in range(kvh):
        sl = slice(g * hd, (g + 1) * hd)
        k_loc = jnp.concatenate([kp_ref[0, :, sl], ko_ref[0, :, sl], kn_ref[0, :, sl]], axis=0)
        v_loc = jnp.concatenate([vp_ref[0, :, sl], vo_ref[0, :, sl], vn_ref[0, :, sl]], axis=0)
        k_ctx = ck_ref[0, :, sl].astype(BF16)
        v_ctx = cv_ref[0, :, sl].astype(BF16)
        for r in range(group):
            h = g * group + r
            qh = q_ref[0, :, h * hd:(h + 1) * hd]
            parts = [(_dot_nt(qh, k_loc) + bias, v_loc), (_dot_nt(qh, k_ctx), v_ctx)]
            o_ref[0, :, h * hd:(h + 1) * hd] = _sink_softmax_pv(parts, sink_ref[h] * LOG2E).astype(o_ref.dtype)


def _attn_lat(q, k, v, ck, cv, sink, kvh, group, hd, blk):
    bsz, seq, nq = q.shape
    nk = k.shape[-1]
    nctx = ck.shape[1]
    nb = seq // blk
    prev = pl.BlockSpec((1, blk, nk), lambda b, n: (b, jnp.maximum(n - 1, 0), 0))
    own = pl.BlockSpec((1, blk, nk), lambda b, n: (b, n, 0))
    nxt = pl.BlockSpec((1, blk, nk), lambda b, n: (b, jnp.minimum(n + 1, nb - 1), 0))
    ctx = pl.BlockSpec((1, nctx, nk), lambda b, n: (b, 0, 0))
    return pl.pallas_call(
        functools.partial(_attn_lat_kernel, kvh=kvh, group=group, hd=hd, blk=blk),
        out_shape=jax.ShapeDtypeStruct((bsz, seq, nq), BF16),
        grid=(bsz, nb),
        in_specs=[pl.BlockSpec(memory_space=pltpu.SMEM), pl.BlockSpec((1, blk, nq), lambda b, n: (b, n, 0)),
                  prev, own, nxt, prev, own, nxt, ctx, ctx],
        out_specs=pl.BlockSpec((1, blk, nq), lambda b, n: (b, n, 0)),
        compiler_params=_cparams("parallel", "parallel"),
        name="attn_lat",
    )(sink, q, k, k, k, v, v, v, ck, cv)


def _ssd_proj_kernel(x_ref, xp_ref, xn_ref, a_ref, sh_ref, wz_ref, wx_ref, wdt_ref, cw_ref, cb_ref,
                     z_ref, xs_ref, bm_ref, cm_ref, dt_ref, xe_ref, *, tm, nl, cn, inner, gs):
    xe_ref[...] = _ext_rows(x_ref, xp_ref, xn_ref, a_ref[0], sh_ref[0], nl).astype(BF16)
    xt = xe_ref[HALO:HALO + tm, :]
    z_ref[0] = _dot(xt, wz_ref[...])
    dt_ref[0] = _dot(xt, wdt_ref[...])
    for j in range((inner + 2 * gs) // cn):
        sl = slice(j * cn, (j + 1) * cn)
        h = _dot(xe_ref[...], wx_ref[:, sl])
        y = _silu(_dwconv_rows(h, cw_ref[:, sl], cb_ref[:, sl], 5))[HALO:HALO + tm]
        if j * cn < inner:
            xs_ref[0, :, sl] = y
        elif j * cn < inner + gs:
            bm_ref[0, :, j * cn - inner:(j + 1) * cn - inner] = y
        else:
            cm_ref[0, :, j * cn - inner - gs:(j + 1) * cn - inner - gs] = y


def _ssd_project(x, a, sh, w_in, conv_w, conv_b, inner, gs, ndt, tm):
    bsz, seq, d = x.shape
    nl = seq // tm
    cn = 512
    wz = w_in[:, :inner].astype(BF16)
    wx = w_in[:, inner:2 * inner + 2 * gs].astype(BF16)
    wdt = w_in[:, 2 * inner + 2 * gs:].astype(BF16)
    mod_spec = pl.BlockSpec((1, 1, d), lambda b, l: (b, 0, 0))
    sds = lambda n: jax.ShapeDtypeStruct((bsz, seq, n), F32)
    return pl.pallas_call(
        functools.partial(_ssd_proj_kernel, tm=tm, nl=nl, cn=cn, inner=inner, gs=gs),
        out_shape=(sds(inner), sds(inner), sds(gs), sds(gs), sds(ndt)),
        grid=(bsz, nl),
        in_specs=_halo_specs(tm, d, nl) + [mod_spec, mod_spec, _const_spec(wz.shape), _const_spec(wx.shape),
                                           _const_spec(wdt.shape), _const_spec(conv_w.shape),
                                           _const_spec((1, conv_b.shape[0]))],
        out_specs=(_row_spec(tm, inner), _row_spec(tm, inner), _row_spec(tm, gs), _row_spec(tm, gs),
                   _row_spec(tm, ndt)),
        scratch_shapes=[pltpu.VMEM((tm + 2 * HALO, d), BF16)],
        compiler_params=_cparams("parallel", "parallel"),
        name="ssd_proj",
    )(x, x, x, a, sh, wz, wx, wdt, conv_w, conv_b[None])


def _tri_masks(reverse):
    i = lax.broadcasted_iota(jnp.int32, (CHUNK, CHUNK), 0)
    j = lax.broadcasted_iota(jnp.int32, (CHUNK, CHUNK), 1)
    tri = (j >= i) if reverse else (j <= i)
    tri_t = (j <= i) if reverse else (j >= i)
    return tri, jnp.where(tri, 1.0, 0.0).astype(F32), jnp.where(tri_t, 1.0, 0.0).astype(F32)


def _cumsum_cols(t_bf, a):
    a1 = a.astype(BF16)
    r1 = a - a1.astype(F32)
    a2 = r1.astype(BF16)
    a3 = (r1 - a2.astype(F32)).astype(BF16)
    return _dot(t_bf, a1) + _dot(t_bf, a2) + _dot(t_bf, a3)


def _ssd_chunk(x_ref, b_ref, c_ref, dtc_raw, dtr_raw, bias_c, bias_r, a_c, a_r, state_ref, y_ref, reverse, hp, ns):
    heads = dtc_raw.shape[1]
    groups = b_ref.shape[2] // ns
    nr = heads // groups
    tri, t_mat, t_mat_t = _tri_masks(reverse)
    dt_c = _softplus(dtc_raw + bias_c)
    dt_r = _softplus(dtr_raw + bias_r)
    acs_c = _cumsum_cols(t_mat.astype(BF16), dt_c * a_c) * LOG2E
    acs_r = _dot_exact(dt_r * a_r, t_mat_t) * LOG2E
    end = 0 if reverse else CHUNK - 1
    last_r = acs_r[:, end:end + 1]
    w_r = dt_r * jnp.exp2(last_r - acs_r)
    e_last = jnp.exp2(last_r)
    for g in range(groups):
        bmat = b_ref[0, :, g * ns:(g + 1) * ns]
        cmat = c_ref[0, :, g * ns:(g + 1) * ns]
        cb = _dot_nt(cmat.astype(BF16), bmat.astype(BF16))
        bt = bmat.T
        for r in range(nr):
            h = g * nr + r
            bc = jnp.broadcast_to(acs_c[:, h:h + 1], (CHUNK, CHUNK))
            bc_n = bc if ns == CHUNK else jnp.broadcast_to(acs_c[:, h:h + 1], (CHUNK, ns))
            mp = cb * jnp.exp2(jnp.where(tri, bc - acs_r[h:h + 1, :], NEG_INF)) * dt_r[h:h + 1, :]
            cp = cmat * jnp.exp2(bc_n)
            st = state_ref[h]
            xh = x_ref[0, :, h * hp:(h + 1) * hp].astype(BF16)
            lhs = jnp.concatenate([mp, cp], axis=1).astype(BF16)
            rhs = jnp.concatenate([xh, st.astype(BF16)], axis=0)
            y_ref[0, :, h * hp:(h + 1) * hp] = _dot(lhs, rhs)
            state_ref[h] = st * e_last[h:h + 1, :] + _dot((bt * w_r[h:h + 1, :]).astype(BF16), xh)


def _ssd_scan_kernel(xf_ref, bf_ref, cf_ref, dcf_ref, drf_ref, xb_ref, bb_ref, cb_ref, dcb_ref, drb_ref,
                     biasc_ref, biasr_ref, ac_ref, ar_ref, h0_ref, yf_ref, yb_ref, hout_ref, state_ref, *, hp, ns):
    c = pl.program_id(1)

    @pl.when(c == 0)
    def _():
        state_ref[...] = h0_ref[0]

    _ssd_chunk(xf_ref, bf_ref, cf_ref, dcf_ref[0, 0], drf_ref[0, 0], biasc_ref[0], biasr_ref[0], ac_ref[0], ar_ref[0],
               state_ref.at[0], yf_ref, False, hp, ns)
    _ssd_chunk(xb_ref, bb_ref, cb_ref, dcb_ref[0, 0], drb_ref[0, 0], biasc_ref[1], biasr_ref[1], ac_ref[1], ar_ref[1],
               state_ref.at[1], yb_ref, True, hp, ns)

    @pl.when(c == pl.num_programs(1) - 1)
    def _():
        hout_ref[0] = state_ref[...]


def _ssd_scan(xs, bm, cm, dt_raw, dt_bias, a_log, h0, groups, hp):
    bsz, seq, inner = xs.shape
    heads = inner // hp
    nr = heads // groups
    ns = bm.shape[-1] // groups
    nc = seq // CHUNK
    gn = groups * ns
    dt4 = dt_raw.reshape(bsz, seq, 2, heads)
    dt_col = dt4.transpose(0, 2, 1, 3)
    dt_row = dt4.transpose(0, 2, 3, 1)
    bias = dt_bias.astype(F32).reshape(2, heads)
    a_neg = (-jnp.exp(a_log.astype(F32))).reshape(2, heads)
    h0t = h0.transpose(0, 1, 2, 4, 3)

    def fwd(w):
        return pl.BlockSpec((1, CHUNK, w), lambda b, c: (b, c, 0))

    def bwd(w):
        return pl.BlockSpec((1, CHUNK, w), lambda b, c: (b, nc - 1 - c, 0))

    vec_c = _const_spec((2, 1, heads))
    vec_r = _const_spec((2, heads, 1))
    state_spec = pl.BlockSpec((1, 2, heads, ns, hp), lambda b, c: (b, 0, 0, 0, 0))
    yf, yb, hout = pl.pallas_call(
        functools.partial(_ssd_scan_kernel, hp=hp, ns=ns),
        out_shape=(jax.ShapeDtypeStruct(xs.shape, F32), jax.ShapeDtypeStruct(xs.shape, F32),
                   jax.ShapeDtypeStruct(h0t.shape, F32)),
        grid=(bsz, nc),
        in_specs=[fwd(inner), fwd(gn), fwd(gn),
                  pl.BlockSpec((1, 1, CHUNK, heads), lambda b, c: (b, 0, c, 0)),
                  pl.BlockSpec((1, 1, heads, CHUNK), lambda b, c: (b, 0, 0, c)),
                  bwd(inner), bwd(gn), bwd(gn),
                  pl.BlockSpec((1, 1, CHUNK, heads), lambda b, c: (b, 1, nc - 1 - c, 0)),
                  pl.BlockSpec((1, 1, heads, CHUNK), lambda b, c: (b, 1, 0, nc - 1 - c)),
                  vec_c, vec_r, vec_c, vec_r, state_spec],
        out_specs=(fwd(inner), bwd(inner), state_spec),
        scratch_shapes=[pltpu.VMEM((2, heads, ns, hp), F32)],
        compiler_params=_cparams("parallel", "arbitrary"),
        name="ssd_scan",
    )(xs, bm, cm, dt_col, dt_row, xs, bm, cm, dt_col, dt_row,
      bias[:, None, :], bias[:, :, None], a_neg[:, None, :], a_neg[:, :, None], h0t)
    return yf, yb, hout.transpose(0, 1, 2, 4, 3)


def _ssd_out_prologue(yf_ref, yb_ref, xs_ref, z_ref, d_ref, g_ref):
    y = yf_ref[0] + yb_ref[0] + xs_ref[0] * d_ref[...]
    y = y * _silu(z_ref[0])
    ms = jnp.mean(y * y, axis=-1, keepdims=True)
    return y * lax.rsqrt(ms + EPS) * g_ref[...]


def _diff_attn_kernel(q_ref, k_ref, vt_ref, lam_ref, gsub_ref, o_ref, *, hd, tk):
    q = q_ref[0].astype(F32)
    lane = lax.broadcasted_iota(jnp.int32, q.shape, 1)
    qz = [jnp.where(lane < hd, q, 0.0).astype(BF16), jnp.where(lane >= hd, q, 0.0).astype(BF16)]
    m, l, acc = [None, None], [None, None], [None, None]
    for j in range(k_ref.shape[1] // tk):
        kb = k_ref[0, j * tk:(j + 1) * tk, :]
        vtb = vt_ref[0, 0, :, j * tk:(j + 1) * tk]
        for c in range(2):
            st = _dot_nt(kb, qz[c])
            mx = jnp.max(st, axis=0, keepdims=True)
            if j == 0:
                m[c] = mx
                p = jnp.exp2(st - mx)
                l[c] = jnp.sum(p, axis=0, keepdims=True)
                acc[c] = _dot(vtb, p.astype(BF16))
            else:
                m_new = jnp.maximum(m[c], mx)
                alpha = jnp.exp2(m[c] - m_new)
                p = jnp.exp2(st - m_new)
                l[c] = alpha * l[c] + jnp.sum(p, axis=0, keepdims=True)
                acc[c] = alpha * acc[c] + _dot(vtb, p.astype(BF16))
                m[c] = m_new
    lv = lam_ref[...]
    f1 = jnp.exp(jnp.sum(lv[0:1] * lv[1:2], axis=1, keepdims=True))
    f2 = jnp.exp(jnp.sum(lv[2:3] * lv[3:4], axis=1, keepdims=True))
    lam = f1 - f2 + DIFF_LAMBDA_INIT
    o = (acc[0] / l[0] - lam * (acc[1] / l[1])).T
    ms = jnp.mean(o * o, axis=-1, keepdims=True)
    o_ref[0] = (o * lax.rsqrt(ms + EPS) * gsub_ref[...] * (1.0 - DIFF_LAMBDA_INIT)).astype(o_ref.dtype)


def _diff_attn(q, k, vt, lam_vecs, g_sub, heads, hd, tq, tk):
    bsz, lq, n = q.shape
    lk = k.shape[1]
    vd = 2 * hd
    return pl.pallas_call(
        functools.partial(_diff_attn_kernel, hd=hd, tk=tk),
        out_shape=jax.ShapeDtypeStruct((bsz, lq, n), BF16),
        grid=(bsz, heads, lq // tq),
        in_specs=[pl.BlockSpec((1, tq, vd), lambda b, h, i: (b, i, h)),
                  pl.BlockSpec((1, lk, vd), lambda b, h, i: (b, 0, h)),
                  pl.BlockSpec((1, 1, vd, lk), lambda b, h, i: (b, h, 0, 0)),
                  pl.BlockSpec(lam_vecs.shape, lambda b, h, i: (0, 0)),
                  pl.BlockSpec((1, vd), lambda b, h, i: (0, 0))],
        out_specs=pl.BlockSpec((1, tq, vd), lambda b, h, i: (b, i, h)),
        compiler_params=_cparams("parallel", "parallel", "parallel"),
        name="diff_attn",
    )(q, k, vt, lam_vecs, g_sub[None])


def _ml_proj_kernel(x_ref, xp_ref, xn_ref, a_ref, sh_ref, wm_ref, wo_ref, cw_ref, cb_ref, wq_ref, wk_ref, wv_ref,
                    wif_ref, bif_ref, q_ref, k_ref, v_ref, g_ref, op_ref, xe_ref, *, tm, nl, heads, ihd, dk):
    xe_ref[...] = _ext_rows(x_ref, xp_ref, xn_ref, a_ref[0], sh_ref[0], nl).astype(BF16)
    op_ref[0] = _dot(xe_ref[HALO:HALO + tm, :], wo_ref[...])
    gates = jnp.zeros((tm, bif_ref.shape[1]), F32) + bif_ref[...]
    for h in range(heads):
        sl = slice(h * ihd, (h + 1) * ihd)
        xm_e = _dot(xe_ref[...], wm_ref[:, sl])
        xc = _silu(_dwconv_rows(xm_e, cw_ref[:, sl], cb_ref[:, sl], 5))[HALO:HALO + tm].astype(BF16)
        xm = xm_e[HALO:HALO + tm].astype(BF16)
        q_ref[0, :, h * dk:(h + 1) * dk] = _dot(xc, wq_ref[h]).astype(q_ref.dtype)
        k_ref[0, :, h * dk:(h + 1) * dk] = (_dot(xc, wk_ref[h]) * (dk ** -0.5)).astype(k_ref.dtype)
        v_ref[0, :, sl] = _dot(xm, wv_ref[h]).astype(v_ref.dtype)
        gates = gates + _dot(xc, wif_ref[sl, :])
    g_ref[0] = gates


def _ml_project(x, a, sh, w_in, conv_w, conv_b, w_q, w_k, w_v, w_if, b_if, tm):
    bsz, seq, d = x.shape
    heads, ihd, dk = w_q.shape
    inner = heads * ihd
    nl = seq // tm
    ng = w_if.shape[1]
    wm = w_in[:, :inner].astype(BF16)
    wo = w_in[:, inner:].astype(BF16)
    mod_spec = pl.BlockSpec((1, 1, d), lambda b, l: (b, 0, 0))
    sds = lambda n, dt: jax.ShapeDtypeStruct((bsz, seq, n), dt)
    return pl.pallas_call(
        functools.partial(_ml_proj_kernel, tm=tm, nl=nl, heads=heads, ihd=ihd, dk=dk),
        out_shape=(sds(heads * dk, BF16), sds(heads * dk, BF16), sds(inner, BF16), sds(ng, F32), sds(inner, F32)),
        grid=(bsz, nl),
        in_specs=_halo_specs(tm, d, nl) + [mod_spec, mod_spec, _const_spec(wm.shape), _const_spec(wo.shape),
                                           _const_spec(conv_w.shape), _const_spec((1, inner)),
                                           _const_spec(w_q.shape), _const_spec(w_k.shape), _const_spec(w_v.shape),
                                           _const_spec(w_if.shape), _const_spec((1, ng))],
        out_specs=(_row_spec(tm, heads * dk), _row_spec(tm, heads * dk), _row_spec(tm, inner), _row_spec(tm, ng),
                   _row_spec(tm, inner)),
        scratch_shapes=[pltpu.VMEM((tm + 2 * HALO, d), BF16)],
        compiler_params=_cparams("parallel", "parallel"),
        name="ml_proj",
    )(x, x, x, a, sh, wm, wo, conv_w, conv_b[None], w_q.astype(BF16), w_k.astype(BF16), w_v.astype(BF16),
      w_if.astype(BF16), b_if.reshape(1, ng))


def _ml_chunk(q_ref, k_ref, v_ref, gc, gr, d, c_ref, n_ref, m_ref, h_ref, reverse):
    heads, dk, dv = c_ref.shape
    tri, t_mat, t_mat_t = _tri_masks(reverse)
    bcum_call = _cumsum_cols(t_mat.astype(BF16), _log_sigmoid(gc))
    bcum_rall = _dot_exact(_log_sigmoid(gr), t_mat_t)
    end = 0 if reverse else CHUNK - 1
    for hh in range(heads):
        ci, cf = d * 2 * heads + hh, d * 2 * heads + heads + hh
        q = q_ref[0, :, hh * dk:(hh + 1) * dk]
        k = k_ref[0, :, hh * dk:(hh + 1) * dk]
        v = v_ref[0, :, hh * dv:(hh + 1) * dv]
        bcum_c = bcum_call[:, cf:cf + 1]
        bcum_r = bcum_rall[cf:cf + 1, :]
        i_c = gc[:, ci:ci + 1]
        i_r = gr[ci:ci + 1, :]
        m_old = m_ref[hh]
        cmat = c_ref[hh]
        nrow = n_ref[hh]
        dmat = jnp.where(tri, bcum_c - bcum_r + i_r, NEG_INF)
        inter = bcum_c + m_old
        mt = jnp.maximum(jnp.max(dmat, axis=1, keepdims=True), inter)
        s = _dot_nt(q, k) * jnp.exp(dmat - mt)
        w_int = jnp.exp(inter - mt)
        num = _dot(s.astype(BF16), v) + w_int * _dot(q, cmat.astype(BF16))
        den = jnp.sum(s, axis=1, keepdims=True) + w_int * jnp.sum(q.astype(F32) * nrow, axis=1, keepdims=True)
        h_ref[0, :, hh * dv:(hh + 1) * dv] = num / jnp.maximum(jnp.abs(den), jnp.exp(-mt))
        btot = bcum_c[end:end + 1]
        g = btot - bcum_c + i_c
        m_new = jnp.maximum(btot + m_old, jnp.max(g, axis=0, keepdims=True))
        decay = jnp.exp(btot + m_old - m_new)
        kw = k.astype(F32) * jnp.exp(g - m_new)
        c_ref[hh] = cmat * decay + _dot(kw.T.astype(BF16), v)
        n_ref[hh] = nrow * decay + jnp.sum(kw, axis=0, keepdims=True)
        m_ref[hh] = m_new


def _ml_scan_kernel(qf_ref, kf_ref, vf_ref, gcf_ref, grf_ref, qb_ref, kb_ref, vb_ref, gcb_ref, grb_ref,
                    c0_ref, n0_ref, m0_ref, hf_ref, hb_ref, cout_ref, nout_ref, mout_ref, c_ref, n_ref, m_ref):
    c = pl.program_id(1)

    @pl.when(c == 0)
    def _():
        c_ref[...] = c0_ref[0]
        n_ref[...] = n0_ref[0]
        m_ref[...] = m0_ref[0]

    _ml_chunk(qf_ref, kf_ref, vf_ref, gcf_ref[0], grf_ref[0], 0, c_ref.at[0], n_ref.at[0], m_ref.at[0], hf_ref, False)
    _ml_chunk(qb_ref, kb_ref, vb_ref, gcb_ref[0], grb_ref[0], 1, c_ref.at[1], n_ref.at[1], m_ref.at[1], hb_ref, True)

    @pl.when(c == pl.num_programs(1) - 1)
    def _():
        cout_ref[0] = c_ref[...]
        nout_ref[0] = n_ref[...]
        mout_ref[0] = m_ref[...]


def _ml_scan(q, k, v, gates, c0, n0, m0):
    bsz, seq, _ = q.shape
    _, _, heads, dk, dv = c0.shape
    ng = gates.shape[-1]
    nc = seq // CHUNK
    g_row = gates.transpose(0, 2, 1)
    n0 = n0.reshape(bsz, 2, heads, 1, dk)
    m0 = m0.reshape(bsz, 2, heads, 1, 1)

    def fwd(w):
        return pl.BlockSpec((1, CHUNK, w), lambda b, c: (b, c, 0))

    def bwd(w):
        return pl.BlockSpec((1, CHUNK, w), lambda b, c: (b, nc - 1 - c, 0))

    c_spec = pl.BlockSpec((1, 2, heads, dk, dv), lambda b, c: (b, 0, 0, 0, 0))
    n_spec = pl.BlockSpec((1, 2, heads, 1, dk), lambda b, c: (b, 0, 0, 0, 0))
    m_spec = pl.BlockSpec((1, 2, heads, 1, 1), lambda b, c: (b, 0, 0, 0, 0))
    hf, hb, cout, nout, mout = pl.pallas_call(
        _ml_scan_kernel,
        out_shape=(jax.ShapeDtypeStruct(v.shape, F32), jax.ShapeDtypeStruct(v.shape, F32),
                   jax.ShapeDtypeStruct(c0.shape, F32), jax.ShapeDtypeStruct(n0.shape, F32),
                   jax.ShapeDtypeStruct(m0.shape, F32)),
        grid=(bsz, nc),
        in_specs=[fwd(heads * dk), fwd(heads * dk), fwd(heads * dv), fwd(ng),
                  pl.BlockSpec((1, ng, CHUNK), lambda b, c: (b, 0, c)),
                  bwd(heads * dk), bwd(heads * dk), bwd(heads * dv), bwd(ng),
                  pl.BlockSpec((1, ng, CHUNK), lambda b, c: (b, 0, nc - 1 - c)),
                  c_spec, n_spec, m_spec],
        out_specs=(fwd(heads * dv), bwd(heads * dv), c_spec, n_spec, m_spec),
        scratch_shapes=[pltpu.VMEM((2, heads, dk, dv), F32), pltpu.VMEM((2, heads, 1, dk), F32),
                        pltpu.VMEM((2, heads, 1, 1), F32)],
        compiler_params=_cparams("parallel", "arbitrary"),
        name="ml_scan",
    )(q, k, v, gates, g_row, q, k, v, gates, g_row, c0, n0, m0)
    return hf, hb, cout, nout.reshape(bsz, 2, heads, dk), mout.reshape(bsz, 2, heads)


def _ml_out_prologue(hf_ref, hb_ref, op_ref, g_ref, *, heads, dv):
    h = hf_ref[0] + hb_ref[0]
    outs = []
    for hh in range(heads):
        blk = h[:, hh * dv:(hh + 1) * dv]
        ms = jnp.mean(blk * blk, axis=-1, keepdims=True)
        outs.append(blk * lax.rsqrt(ms + EPS))
    return jax.nn.sigmoid(op_ref[0]) * (jnp.concatenate(outs, axis=1) * g_ref[...])


def _stream_mods(mod, g1, g2):
    a1 = (g1 * (1.0 + mod[:, 1]))[:, None]
    a2 = (g2 * (1.0 + mod[:, 4]))[:, None]
    return (a1, mod[:, 0][:, None], mod[:, 2][:, None]), (a2, mod[:, 3][:, None], mod[:, 5][:, None])


def _tile(seq):
    return min(seq, 512)


def _attn_layer(x, mods, lat_cache, w_qkv, g_q, g_k, sink, w_o):
    a, sh, gt = mods
    bsz, seq, d = x.shape
    hd = g_q.shape[0]
    heads = sink.shape[0]
    kvh = (w_qkv.shape[1] // hd - heads) // 2
    tm = _tile(seq)
    gq = jnp.tile(g_q, LANES // hd)[None]
    gk = jnp.tile(g_k, LANES // hd)[None]
    qscale = hd ** -0.5 * LOG2E
    if lat_cache is None:
        q, k, v = _qkv_project(x, a, sh, w_qkv, gq, gk, heads * hd, kvh * hd, kvh * hd, None, F32, tm, qscale)
        o = _attn_ctx(q, k, v, sink, kvh, heads // kvh, hd)
    else:
        ck, cv = lat_cache
        q, k, v = _qkv_project(x, a, sh, w_qkv, gq, gk, heads * hd, kvh * hd, kvh * hd, _rope_tables(seq, hd), BF16, tm,
                               qscale)
        o = _attn_lat(q, k, v, ck.reshape(bsz, -1, kvh * hd), cv.reshape(bsz, -1, kvh * hd), sink, kvh,
                      heads // kvh, hd, 128)
    x = _oproj(lambda o_ref: o_ref[0], [o], [_row_spec(tm, heads * hd)], x, gt, w_o, tm)
    return x, k.reshape(bsz, seq, kvh, hd), v.reshape(bsz, seq, kvh, hd)


def _ssd_layer(x, mods, h0, w_in, conv_w, conv_b, dt_bias, a_log, d_skip, g_norm, w_out):
    a, sh, gt = mods
    bsz, seq, d = x.shape
    heads = d_skip.shape[0]
    inner = g_norm.shape[0]
    hp = inner // heads
    gs = (conv_w.shape[1] - inner) // 2
    groups = gs // LANES
    tm = _tile(seq)
    z, xs, bm, cm, dt_raw = _ssd_project(x, a, sh, w_in, conv_w, conv_b, inner, gs, 2 * heads, tm)
    yf, yb, hout = _ssd_scan(xs, bm, cm, dt_raw, dt_bias, a_log, h0, groups, hp)
    rs = _row_spec(tm, inner)
    x = _oproj(_ssd_out_prologue, [yf, yb, xs, z, jnp.repeat(d_skip, hp)[None], g_norm[None]],
               [rs, rs, rs, rs, _const_spec((1, inner)), _const_spec((1, inner))], x, gt, w_out, tm)
    return x, hout


def _diff_layer(x, mods, lat_cache, w_qkv, g_q, g_k, lam_vecs, g_sub, w_o):
    a, sh, gt = mods
    bsz, seq, d = x.shape
    hd = g_q.shape[1]
    vd = g_sub.shape[0]
    heads = w_qkv.shape[1] // (4 * hd + vd)
    nq = heads * 2 * hd
    tm = _tile(seq)
    gq = g_q.reshape(1, 2 * hd)
    gk = g_k.reshape(1, 2 * hd)
    qscale = hd ** -0.5 * math.log2(math.e)
    if lat_cache is None:
        q, k, v = _qkv_project(x, a, sh, w_qkv, gq, gk, nq, nq, heads * vd, None, F32, tm, qscale)
        vt = v.astype(BF16).reshape(bsz, seq, heads, vd).transpose(0, 2, 3, 1)
        o = _diff_attn(q, k.astype(BF16), vt, lam_vecs, g_sub, heads, hd, seq, seq)
    else:
        ck, cv = lat_cache
        q, k, v = _qkv_project(x, a, sh, w_qkv, gq, gk, nq, nq, heads * vd, _rope_tables(seq, hd), BF16, tm, qscale)
        kcat = jnp.concatenate([k, ck.reshape(bsz, -1, nq).astype(BF16)], axis=1)
        vcat = jnp.concatenate([v.reshape(bsz, seq, heads, vd), cv.astype(BF16)], axis=1)
        tk = next(t for t in (1152, 1024, 768, 512, 256, 128) if kcat.shape[1] % t == 0)
        o = _diff_attn(q, kcat, vcat.transpose(0, 2, 3, 1), lam_vecs, g_sub, heads, hd, tm, tk)
    x = _oproj(lambda o_ref: o_ref[0], [o], [_row_spec(tm, heads * vd)], x, gt, w_o, tm)
    return x, k.reshape(bsz, seq, heads, 2, hd), v.reshape(bsz, seq, heads, vd)


def _ml_layer(x, mods, c0, n0, m0, w_in, conv_w, conv_b, w_q, w_k, w_v, w_if, b_if, g_norm, w_out):
    a, sh, gt = mods
    bsz, seq, d = x.shape
    heads, dv = g_norm.shape
    inner = heads * dv
    tm = _tile(seq)
    q, k, v, gates, o_pre = _ml_project(x, a, sh, w_in, conv_w, conv_b, w_q, w_k, w_v, w_if, b_if, tm)
    hf, hb, cout, nout, mout = _ml_scan(q, k, v, gates, c0, n0, m0)
    rs = _row_spec(tm, inner)
    x = _oproj(functools.partial(_ml_out_prologue, heads=heads, dv=dv), [hf, hb, o_pre, g_norm.reshape(1, inner)],
               [rs, rs, rs, _const_spec((1, inner))], x, gt, w_out, tm)
    return x, cout, nout, mout


def kernel(x_prompt, x_sample, cache_attn_k, cache_attn_v, state_ssd, cache_diff_k, cache_diff_v, state_mlstm_c, state_mlstm_n, state_mlstm_m, c, c_ctx, ada_w, ada_b, norm1_g, norm2_g, ffn_w_up, ffn_conv_w, ffn_conv_b, ffn_w_down, attn_w_qkv, attn_g_q, attn_g_k, attn_sink, attn_w_o, ssd_w_in, ssd_conv_w, ssd_conv_b, ssd_dt_bias, ssd_a_log, ssd_d, ssd_g_norm, ssd_w_out, diff_w_qkv, diff_g_q, diff_g_k, diff_lq1, diff_lk1, diff_lq2, diff_lk2, diff_g_sub, diff_w_o, ml_w_in, ml_conv_w, ml_conv_b, ml_w_q, ml_w_k, ml_w_v, ml_w_if, ml_b_if, ml_g_norm, ml_w_out):
    xp, xs = x_prompt, x_sample
    bp, bs = xp.shape[0], xs.shape[0]
    d = xp.shape[-1]
    depth = ada_w.shape[0]
    rows = 16
    cond = jnp.concatenate([c_ctx[None], c, jnp.zeros((rows - 1 - bs, d), F32)], axis=0)
    mod = _ada(cond, ada_w, ada_b).reshape(depth, rows, 6, d)
    lam_vecs = jnp.stack([diff_lq1, diff_lk1, diff_lq2, diff_lk2])
    outs = {}
    for i in range(depth):
        mp1, mp2 = _stream_mods(jnp.broadcast_to(mod[i, 0], (bp, 6, d)), norm1_g[i], norm2_g[i])
        ms1, ms2 = _stream_mods(mod[i, 1:1 + bs], norm1_g[i], norm2_g[i])
        kind = i % 4
        if kind == 0:
            xp, outs["ak"], outs["av"] = _attn_layer(xp, mp1, None, attn_w_qkv, attn_g_q, attn_g_k, attn_sink, attn_w_o)
            xs, _, _ = _attn_layer(xs, ms1, (cache_attn_k, cache_attn_v), attn_w_qkv, attn_g_q, attn_g_k, attn_sink,
                                   attn_w_o)
        elif kind == 1:
            ssd_w = (ssd_w_in, ssd_conv_w, ssd_conv_b, ssd_dt_bias, ssd_a_log, ssd_d, ssd_g_norm, ssd_w_out)
            xp, outs["ssd"] = _ssd_layer(xp, mp1, jnp.zeros((bp,) + state_ssd.shape[1:], F32), *ssd_w)
            xs, _ = _ssd_layer(xs, ms1, state_ssd, *ssd_w)
        elif kind == 2:
            diff_w = (diff_w_qkv, diff_g_q, diff_g_k, lam_vecs, diff_g_sub, diff_w_o)
            xp, outs["dk"], outs["dv"] = _diff_layer(xp, mp1, None, *diff_w)
            xs, _, _ = _diff_layer(xs, ms1, (cache_diff_k, cache_diff_v), *diff_w)
        else:
            ml_w = (ml_w_in, ml_conv_w, ml_conv_b, ml_w_q, ml_w_k, ml_w_v, ml_w_if, ml_b_if, ml_g_norm, ml_w_out)
            zc = jnp.zeros((bp,) + state_mlstm_c.shape[1:], F32)
            zn = jnp.zeros((bp,) + state_mlstm_n.shape[1:], F32)
            zm = jnp.zeros((bp,) + state_mlstm_m.shape[1:], F32)
            xp, outs["mc"], outs["mn"], outs["mm"] = _ml_layer(xp, mp1, zc, zn, zm, *ml_w)
            xs, _, _, _ = _ml_layer(xs, ms1, state_mlstm_c, state_mlstm_n, state_mlstm_m, *ml_w)
        ffn_w = (ffn_w_up[i], ffn_conv_w[i], ffn_conv_b[i], ffn_w_down[i])
        xp = _ffn(xp, *mp2, *ffn_w, tm=_tile(xp.shape[1]))
        xs = _ffn(xs, *ms2, *ffn_w, tm=_tile(xs.shape[1]))
    return (xp, xs, outs["ak"], outs["av"], outs["ssd"], outs["dk"], outs["dv"], outs["mc"], outs["mn"], outs["mm"])
```

```python
import functools
import math

import jax
import jax.numpy as jnp
from jax import lax
from jax.experimental import pallas as pl
from jax.experimental.pallas import tpu as pltpu

F32 = jnp.float32
BF16 = jnp.bfloat16
HIGHEST = lax.Precision.HIGHEST

EPS = 1e-6
ROPE_THETA = 10000.0
GRID_W = 64
DIFF_LAMBDA_INIT = 0.8 - 0.6 * math.exp(-0.3 * 2)

VMEM_LIMIT_BYTES = 56 * 1024 * 1024
LANES = 128
MXU_ROWS = 256
HALO = 16
CHUNK = 128
NEG_INF = float("-inf")
LOG2E = math.log2(math.e)


def _cparams(*sem):
    return pltpu.CompilerParams(dimension_semantics=sem, vmem_limit_bytes=VMEM_LIMIT_BYTES)


def _dot(a, b):
    return jnp.dot(a, b, preferred_element_type=F32)


def _dot_nt(a, b):
    return lax.dot_general(a, b, (((1,), (1,)), ((), ())), preferred_element_type=F32)


def _dot_exact(a, b):
    return jnp.dot(a, b, preferred_element_type=F32, precision=HIGHEST)


def _norm_mod(x, a, sh):
    ms = jnp.mean(x * x, axis=-1, keepdims=True)
    return x * lax.rsqrt(ms + EPS) * a + sh


def _silu(x):
    return x * jax.nn.sigmoid(x)


def _softplus(x):
    return jnp.maximum(x, 0.0) + jnp.log(1.0 + jnp.exp(-jnp.abs(x)))


def _log_sigmoid(x):
    return jnp.minimum(x, 0.0) - jnp.log(1.0 + jnp.exp(-jnp.abs(x)))


def _const_spec(shape):
    nd = len(shape)
    return pl.BlockSpec(shape, lambda *_: (0,) * nd)


def _ext_rows(x_ref, xp_ref, xn_ref, a, sh, nl):
    l = pl.program_id(1)
    xt = _norm_mod(x_ref[0], a, sh)
    xp = _norm_mod(xp_ref[0], a, sh) * (l > 0).astype(F32)
    xn = _norm_mod(xn_ref[0], a, sh) * (l < nl - 1).astype(F32)
    return jnp.concatenate([xp, xt, xn], axis=0)


def _halo_specs(tm, d, nl):
    r = tm // HALO
    return [
        pl.BlockSpec((1, tm, d), lambda b, l: (b, l, 0)),
        pl.BlockSpec((1, HALO, d), lambda b, l: (b, jnp.maximum(l * r - 1, 0), 0)),
        pl.BlockSpec((1, HALO, d), lambda b, l: (b, jnp.minimum((l + 1) * r, nl * r - 1), 0)),
    ]


def _row_blocks(tm):
    n = max(tm // MXU_ROWS, 1)
    return [slice(i * (tm // n), (i + 1) * (tm // n)) for i in range(n)]


def _dwconv_rows(h, w, b, width):
    rows = h.shape[0]
    pad = width // 2
    y = h * w[pad:pad + 1]
    for k in range(width):
        if k == pad:
            continue
        y = y + pltpu.roll(h, (pad - k) % rows, 0) * w[k:k + 1]
    return y + b


def _ada_kernel(c_ref, w_ref, b_ref, o_ref):
    s = _silu(c_ref[...]).astype(BF16)
    o_ref[0] = _dot(s, w_ref[0].astype(BF16)) + b_ref[0]


def _ada(cond, ada_w, ada_b):
    depth, d, n = ada_w.shape
    r = cond.shape[0]
    tn = 1536
    return pl.pallas_call(
        _ada_kernel,
        out_shape=jax.ShapeDtypeStruct((depth, r, n), F32),
        grid=(depth, n // tn),
        in_specs=[
            pl.BlockSpec((r, d), lambda i, j: (0, 0)),
            pl.BlockSpec((1, d, tn), lambda i, j: (i, 0, j)),
            pl.BlockSpec((1, 1, tn), lambda i, j: (i, 0, j)),
        ],
        out_specs=pl.BlockSpec((1, r, tn), lambda i, j: (i, 0, j)),
        compiler_params=_cparams("parallel", "parallel"),
        name="ada",
    )(cond, ada_w, ada_b.reshape(depth, 1, n))


def _ffn_kernel(x_ref, xp_ref, xn_ref, a_ref, sh_ref, gt_ref, wa_ref, wg_ref, cwa_ref, cwg_ref, cba_ref, cbg_ref,
                wd_ref, o_ref, xe_ref, acc_ref, *, tm, nl, nchunk):
    xe_ref[...] = _ext_rows(x_ref, xp_ref, xn_ref, a_ref[0], sh_ref[0], nl).astype(BF16)

    def up(j):
        xe = xe_ref[...]
        return _dot(xe, wa_ref[j]), _dot(xe, wg_ref[j])

    nxt = up(0)
    for j in range(nchunk):
        ua, ug = nxt
        if j + 1 < nchunk:
            nxt = up(j + 1)
        ha = _dwconv_rows(ua, cwa_ref[j], cba_ref[j], 3)[HALO:HALO + tm]
        hg = _dwconv_rows(ug, cwg_ref[j], cbg_ref[j], 3)[HALO:HALO + tm]
        part = _dot((ha * _silu(hg)).astype(BF16), wd_ref[j])
        if j == 0:
            acc_ref[...] = part
        else:
            acc_ref[...] += part
    o_ref[0] = x_ref[0] + gt_ref[0] * acc_ref[...]


def _ffn(x, a, sh, gt, w_up, conv_w, conv_b, w_down, tm):
    bsz, seq, d = x.shape
    dff = w_down.shape[0]
    cn = 256
    nchunk = dff // cn
    nl = seq // tm
    wa = w_up[:, :dff].reshape(d, nchunk, cn).transpose(1, 0, 2).astype(BF16)
    wg = w_up[:, dff:].reshape(d, nchunk, cn).transpose(1, 0, 2).astype(BF16)
    cwa = conv_w[:, :dff].reshape(3, nchunk, cn).transpose(1, 0, 2)
    cwg = conv_w[:, dff:].reshape(3, nchunk, cn).transpose(1, 0, 2)
    cba = conv_b[:dff].reshape(nchunk, 1, cn)
    cbg = conv_b[dff:].reshape(nchunk, 1, cn)
    wd = w_down.reshape(nchunk, cn, d).astype(BF16)
    mod_spec = pl.BlockSpec((1, 1, d), lambda b, l: (b, 0, 0))
    return pl.pallas_call(
        functools.partial(_ffn_kernel, tm=tm, nl=nl, nchunk=nchunk),
        out_shape=jax.ShapeDtypeStruct(x.shape, F32),
        grid=(bsz, nl),
        in_specs=_halo_specs(tm, d, nl) + [mod_spec, mod_spec, mod_spec,
                                       _const_spec(wa.shape), _const_spec(wg.shape), _const_spec(cwa.shape),
                                       _const_spec(cwg.shape), _const_spec(cba.shape), _const_spec(cbg.shape),
                                       _const_spec(wd.shape)],
        out_specs=pl.BlockSpec((1, tm, d), lambda b, l: (b, l, 0)),
        scratch_shapes=[pltpu.VMEM((tm + 2 * HALO, d), BF16), pltpu.VMEM((tm, d), F32)],
        compiler_params=_cparams("parallel", "parallel"),
        name="ffn",
    )(x, x, x, a, sh, gt, wa, wg, cwa, cwg, cba, cbg, wd)


def _oproj_kernel(*refs, prologue, n_in):
    in_refs = refs[:n_in]
    x_ref, gt_ref, w_ref, o_ref = refs[n_in:]
    for rows in _row_blocks(x_ref.shape[1]):
        lhs = prologue(rows, *in_refs)
        o_ref[0, rows] = x_ref[0, rows] + gt_ref[0] * _dot(lhs.astype(BF16), w_ref[...])


def _oproj(prologue, ins, in_specs, x, gt, w, tm):
    bsz, seq, d = x.shape
    return pl.pallas_call(
        functools.partial(_oproj_kernel, prologue=prologue, n_in=len(ins)),
        out_shape=jax.ShapeDtypeStruct(x.shape, F32),
        grid=(bsz, seq // tm),
        in_specs=list(in_specs) + [pl.BlockSpec((1, tm, d), lambda b, l: (b, l, 0)),
                                   pl.BlockSpec((1, 1, d), lambda b, l: (b, 0, 0)),
                                   _const_spec(w.shape)],
        out_specs=pl.BlockSpec((1, tm, d), lambda b, l: (b, l, 0)),
        compiler_params=_cparams("parallel", "parallel"),
        name="oproj",
    )(*ins, x, gt, w.astype(BF16))


def _rows_prologue(rows, o_ref):
    return o_ref[0, rows]


def _row_spec(tm, n):
    return pl.BlockSpec((1, tm, n), lambda b, l: (b, l, 0))


def _head_sumsq_matrix(hd):
    i = lax.broadcasted_iota(jnp.int32, (LANES, LANES), 0) // hd
    j = lax.broadcasted_iota(jnp.int32, (LANES, LANES), 1) // hd
    return jnp.where(i == j, 1.0 / hd, 0.0).astype(BF16)


def _qk_norm_rope(y, gain, rope, avg):
    outs = []
    for c in range(y.shape[1] // LANES):
        yb = y[:, c * LANES:(c + 1) * LANES]
        ms = _dot((yb * yb).astype(BF16), avg)
        yb = yb * lax.rsqrt(ms + EPS) * gain
        if rope is not None:
            cos, s_lo, s_hi = rope
            yb = yb * cos + pltpu.roll(yb, LANES - 32, 1) * s_lo + pltpu.roll(yb, 32, 1) * s_hi
        outs.append(yb)
    return outs


def _rope_tables(seq, hd):
    rows = seq // GRID_W
    row = jnp.repeat(jnp.arange(rows, dtype=F32), GRID_W)
    col = jnp.tile(jnp.arange(GRID_W, dtype=F32), rows)
    nf = hd // 4
    inv = ROPE_THETA ** (-jnp.arange(nf, dtype=F32) / nf)
    ang = jnp.concatenate([row[:, None] * inv, col[:, None] * inv], axis=-1)
    cos, sin = jnp.cos(ang), jnp.sin(ang)
    zero = jnp.zeros_like(sin)
    rep = LANES // hd
    cos_t = jnp.tile(jnp.concatenate([cos, cos], axis=-1), (1, rep))
    s_lo = jnp.tile(jnp.concatenate([-sin, zero], axis=-1), (1, rep))
    s_hi = jnp.tile(jnp.concatenate([zero, sin], axis=-1), (1, rep))
    return cos_t, s_lo, s_hi


def _attn_qkv_kernel(*refs, use_rope, nq, nk, qscale):
    if use_rope:
        x_ref, a_ref, sh_ref, w_ref, gq_ref, gk_ref, cos_ref, slo_ref, shi_ref, q_ref, k_ref, v_ref = refs
        rope = (cos_ref, slo_ref, shi_ref)
    else:
        x_ref, a_ref, sh_ref, w_ref, gq_ref, gk_ref, q_ref, k_ref, v_ref = refs
        rope = None
    avg = _head_sumsq_matrix(64)
    for rows in _row_blocks(x_ref.shape[1]):
        xn = _norm_mod(x_ref[0, rows], a_ref[0], sh_ref[0]).astype(BF16)
        y = _dot(xn, w_ref[...])
        rope_rows = None if rope is None else tuple(t[rows] for t in rope)
        q = _qk_norm_rope(y[:, :nq], gq_ref[...], rope_rows, avg)
        k = _qk_norm_rope(y[:, nq:nq + nk], gk_ref[...], rope_rows, avg)
        for c, blk in enumerate(q):
            q_ref[0, rows, c * LANES:(c + 1) * LANES] = (blk * qscale).astype(q_ref.dtype)
        for c, blk in enumerate(k):
            k_ref[0, rows, c * LANES:(c + 1) * LANES] = blk.astype(k_ref.dtype)
        v_ref[0, rows] = y[:, nq + nk:].astype(v_ref.dtype)


def _qkv_project(x, a, sh, w, gq, gk, nq, nk, nv, rope, kv_dtype, tm, qscale):
    bsz, seq, d = x.shape
    mod_spec = pl.BlockSpec((1, 1, d), lambda b, l: (b, 0, 0))
    ins = [x, a, sh, w.astype(BF16), gq, gk]
    specs = [_row_spec(tm, d), mod_spec, mod_spec, _const_spec(w.shape), _const_spec(gq.shape), _const_spec(gk.shape)]
    if rope is not None:
        ins += list(rope)
        specs += [pl.BlockSpec((tm, LANES), lambda b, l: (l, 0))] * 3
    return pl.pallas_call(
        functools.partial(_attn_qkv_kernel, use_rope=rope is not None, nq=nq, nk=nk, qscale=qscale),
        out_shape=(jax.ShapeDtypeStruct((bsz, seq, nq), BF16),
                   jax.ShapeDtypeStruct((bsz, seq, nk), kv_dtype),
                   jax.ShapeDtypeStruct((bsz, seq, nv), kv_dtype)),
        grid=(bsz, seq // tm),
        in_specs=specs,
        out_specs=(_row_spec(tm, nq), _row_spec(tm, nk), _row_spec(tm, nv)),
        compiler_params=_cparams("parallel", "parallel"),
        name="qkv",
    )(*ins)


def _sink_softmax_pv(parts, sink2):
    m = sink2
    for s, _ in parts:
        m = jnp.maximum(m, jnp.max(s, axis=1, keepdims=True))
    den = jnp.exp2(sink2 - m)
    acc = None
    for s, v in parts:
        e = jnp.exp2(s - m)
        den = den + jnp.sum(e, axis=1, keepdims=True)
        pv = _dot(e.astype(BF16), v)
        acc = pv if acc is None else acc + pv
    return acc / den


def _attn_ctx_kernel(sink_ref, q_ref, k_ref, v_ref, o_ref, *, kvh, group, hd):
    for g in range(kvh):
        kh = k_ref[0, :, g * hd:(g + 1) * hd].astype(BF16)
        vh = v_ref[0, :, g * hd:(g + 1) * hd].astype(BF16)
        for r in range(group):
            h = g * group + r
            s = _dot_nt(q_ref[0, :, h * hd:(h + 1) * hd], kh)
            o_ref[0, :, h * hd:(h + 1) * hd] = _sink_softmax_pv([(s, vh)], sink_ref[h] * LOG2E).astype(o_ref.dtype)


def _attn_ctx(q, k, v, sink, kvh, group, hd):
    bsz, seq, nq = q.shape
    nk = k.shape[-1]
    return pl.pallas_call(
        functools.partial(_attn_ctx_kernel, kvh=kvh, group=group, hd=hd),
        out_shape=jax.ShapeDtypeStruct((bsz, seq, nq), BF16),
        grid=(bsz,),
        in_specs=[pl.BlockSpec(memory_space=pltpu.SMEM),
                  pl.BlockSpec((1, seq, nq), lambda b: (b, 0, 0)),
                  pl.BlockSpec((1, seq, nk), lambda b: (b, 0, 0)),
                  pl.BlockSpec((1, seq, nk), lambda b: (b, 0, 0))],
        out_specs=pl.BlockSpec((1, seq, nq), lambda b: (b, 0, 0)),
        compiler_params=_cparams("parallel"),
        name="attn_ctx",
    )(sink, q, k, v)


def _attn_lat_kernel(sink_ref, q_ref, kp_ref, ko_ref, kn_ref, vp_ref, vo_ref, vn_ref, ck_ref, cv_ref, o_ref, *,
                     kvh, group, hd, blk):
    n = pl.program_id(1)
    nb = pl.num_programs(1)
    qi = lax.broadcasted_iota(jnp.int32, (blk, 3 * blk), 0)
    kj = lax.broadcasted_iota(jnp.int32, (blk, 3 * blk), 1)
    ok = (kj >= qi) & (kj <= qi + 2 * blk)
    ok = ok & ((kj >= blk) | (n > 0)) & ((kj < 2 * blk) | (n < nb - 1))
    bias = jnp.where(ok, 0.0, NEG_INF)
    def keys_values(g):
        sl = slice(g * hd, (g + 1) * hd)
        k_loc = jnp.concatenate([kp_ref[0, :, sl], ko_ref[0, :, sl], kn_ref[0, :, sl]], axis=0)
        v_loc = jnp.concatenate([vp_ref[0, :, sl], vo_ref[0, :, sl], vn_ref[0, :, sl]], axis=0)
        return k_loc, ck_ref[0, :, sl].astype(BF16), v_loc, cv_ref[0, :, sl].astype(BF16)

    def scores(h, kv):
        qh = q_ref[0, :, h * hd:(h + 1) * hd]
        return _dot_nt(qh, kv[0]) + bias, _dot_nt(qh, kv[1])

    kv = keys_values(0)
    nxt = scores(0, kv)
    for h in range(kvh * group):
        s_loc, s_ctx = nxt
        v_loc, v_ctx = kv[2], kv[3]
        if h + 1 < kvh * group:
            if (h + 1) % group == 0:
                kv = keys_values((h + 1) // group)
            nxt = scores(h + 1, kv)
        o = _sink_softmax_pv([(s_loc, v_loc), (s_ctx, v_ctx)], sink_ref[h] * LOG2E)
        o_ref[0, :, h * hd:(h + 1) * hd] = o.astype(o_ref.dtype)


def _attn_lat(q, k, v, ck, cv, sink, kvh, group, hd, blk):
    bsz, seq, nq = q.shape
    nk = k.shape[-1]
    nctx = ck.shape[1]
    nb = seq // blk
    prev = pl.BlockSpec((1, blk, nk), lambda b, n: (b, jnp.maximum(n - 1, 0), 0))
    own = pl.BlockSpec((1, blk, nk), lambda b, n: (b, n, 0))
    nxt = pl.BlockSpec((1, blk, nk), lambda b, n: (b, jnp.minimum(n + 1, nb - 1), 0))
    ctx = pl.BlockSpec((1, nctx, nk), lambda b, n: (b, 0, 0))
    return pl.pallas_call(
        functools.partial(_attn_lat_kernel, kvh=kvh, group=group, hd=hd, blk=blk),
        out_shape=jax.ShapeDtypeStruct((bsz, seq, nq), BF16),
        grid=(bsz, nb),
        in_specs=[pl.BlockSpec(memory_space=pltpu.SMEM), pl.BlockSpec((1, blk, nq), lambda b, n: (b, n, 0)),
                  prev, own, nxt, prev, own, nxt, ctx, ctx],
        out_specs=pl.BlockSpec((1, blk, nq), lambda b, n: (b, n, 0)),
        compiler_params=_cparams("parallel", "parallel"),
        name="attn_lat",
    )(sink, q, k, k, k, v, v, v, ck, cv)


def _ssd_proj_kernel(x_ref, xp_ref, xn_ref, a_ref, sh_ref, wz_ref, wx_ref, wdt_ref, cw_ref, cb_ref,
                     z_ref, xs_ref, bm_ref, cm_ref, dt_ref, xe_ref, *, tm, nl, cn, inner, gs):
    xe_ref[...] = _ext_rows(x_ref, xp_ref, xn_ref, a_ref[0], sh_ref[0], nl).astype(BF16)
    nchunk = (inner + 2 * gs) // cn

    def in_proj(j):
        return _dot(xe_ref[...], wx_ref[:, j * cn:(j + 1) * cn])

    nxt = in_proj(0)
    xt = xe_ref[HALO:HALO + tm, :]
    z_ref[0] = _dot(xt, wz_ref[...]).astype(z_ref.dtype)
    dt_ref[0] = _dot(xt, wdt_ref[...])
    for j in range(nchunk):
        sl = slice(j * cn, (j + 1) * cn)
        h = nxt
        if j + 1 < nchunk:
            nxt = in_proj(j + 1)
        y = _silu(_dwconv_rows(h, cw_ref[:, sl], cb_ref[:, sl], 5))[HALO:HALO + tm]
        if j * cn < inner:
            xs_ref[0, :, sl] = y.astype(xs_ref.dtype)
        elif j * cn < inner + gs:
            bm_ref[0, :, j * cn - inner:(j + 1) * cn - inner] = y
        else:
            cm_ref[0, :, j * cn - inner - gs:(j + 1) * cn - inner - gs] = y


def _ssd_project(x, a, sh, w_in, conv_w, conv_b, inner, gs, ndt, tm):
    bsz, seq, d = x.shape
    nl = seq // tm
    cn = 512
    wz = w_in[:, :inner].astype(BF16)
    wx = w_in[:, inner:2 * inner + 2 * gs].astype(BF16)
    wdt = w_in[:, 2 * inner + 2 * gs:].astype(BF16)
    mod_spec = pl.BlockSpec((1, 1, d), lambda b, l: (b, 0, 0))
    sds = lambda n, dt: jax.ShapeDtypeStruct((bsz, seq, n), dt)
    return pl.pallas_call(
        functools.partial(_ssd_proj_kernel, tm=tm, nl=nl, cn=cn, inner=inner, gs=gs),
        out_shape=(sds(inner, BF16), sds(inner, BF16), sds(gs, F32), sds(gs, F32), sds(ndt, F32)),
        grid=(bsz, nl),
        in_specs=_halo_specs(tm, d, nl) + [mod_spec, mod_spec, _const_spec(wz.shape), _const_spec(wx.shape),
                                           _const_spec(wdt.shape), _const_spec(conv_w.shape),
                                           _const_spec((1, conv_b.shape[0]))],
        out_specs=(_row_spec(tm, inner), _row_spec(tm, inner), _row_spec(tm, gs), _row_spec(tm, gs),
                   _row_spec(tm, ndt)),
        scratch_shapes=[pltpu.VMEM((tm + 2 * HALO, d), BF16)],
        compiler_params=_cparams("parallel", "parallel"),
        name="ssd_proj",
    )(x, x, x, a, sh, wz, wx, wdt, conv_w, conv_b[None])


def _tri_masks(reverse):
    i = lax.broadcasted_iota(jnp.int32, (CHUNK, CHUNK), 0)
    j = lax.broadcasted_iota(jnp.int32, (CHUNK, CHUNK), 1)
    tri = (j >= i) if reverse else (j <= i)
    tri_t = (j <= i) if reverse else (j >= i)
    return tri, jnp.where(tri, 1.0, 0.0).astype(F32), jnp.where(tri_t, 1.0, 0.0).astype(F32)


def _cumsum_cols(t_bf, a):
    a1 = a.astype(BF16)
    r1 = a - a1.astype(F32)
    a2 = r1.astype(BF16)
    a3 = (r1 - a2.astype(F32)).astype(BF16)
    return _dot(t_bf, a1) + _dot(t_bf, a2) + _dot(t_bf, a3)


def _ssd_chunk(x_ref, b_ref, c_ref, dtc_raw, dtr_raw, bias_c, bias_r, a_c, a_r, state_ref, y_ref, reverse, hp, ns):
    heads = dtc_raw.shape[1]
    groups = b_ref.shape[2] // ns
    nr = heads // groups
    tri, t_mat, t_mat_t = _tri_masks(reverse)
    dt_c = _softplus(dtc_raw + bias_c)
    dt_r = _softplus(dtr_raw + bias_r)
    acs_c = _cumsum_cols(t_mat.astype(BF16), dt_c * a_c) * LOG2E
    acs_r = _dot_exact(dt_r * a_r, t_mat_t) * LOG2E
    end = 0 if reverse else CHUNK - 1
    last_r = acs_r[:, end:end + 1]
    w_r = dt_r * jnp.exp2(last_r - acs_r)
    e_last = jnp.exp2(last_r)
    for g in range(groups):
        bmat = b_ref[0, :, g * ns:(g + 1) * ns]
        cmat = c_ref[0, :, g * ns:(g + 1) * ns]
        cb = _dot_nt(cmat.astype(BF16), bmat.astype(BF16))
        bt = bmat.T
        for r in range(nr):
            h = g * nr + r
            bc = jnp.broadcast_to(acs_c[:, h:h + 1], (CHUNK, CHUNK))
            bc_n = bc if ns == CHUNK else jnp.broadcast_to(acs_c[:, h:h + 1], (CHUNK, ns))
            mp = cb * jnp.exp2(jnp.where(tri, bc - acs_r[h:h + 1, :], NEG_INF)) * dt_r[h:h + 1, :]
            cp = cmat * jnp.exp2(bc_n)
            st = state_ref[h]
            xh = x_ref[0, :, h * hp:(h + 1) * hp]
            lhs = jnp.concatenate([mp, cp], axis=1).astype(BF16)
            rhs = jnp.concatenate([xh, st.astype(BF16)], axis=0)
            y_ref[0, :, h * hp:(h + 1) * hp] = _dot(lhs, rhs).astype(y_ref.dtype)
            state_ref[h] = st * e_last[h:h + 1, :] + _dot((bt * w_r[h:h + 1, :]).astype(BF16), xh)


def _ssd_scan_kernel(xf_ref, bf_ref, cf_ref, dcf_ref, drf_ref, xb_ref, bb_ref, cb_ref, dcb_ref, drb_ref,
                     biasc_ref, biasr_ref, ac_ref, ar_ref, h0_ref, yf_ref, yb_ref, hout_ref, state_ref, *, hp, ns):
    c = pl.program_id(1)

    @pl.when(c == 0)
    def _():
        state_ref[...] = h0_ref[0]

    _ssd_chunk(xf_ref, bf_ref, cf_ref, dcf_ref[0, 0], drf_ref[0, 0], biasc_ref[0], biasr_ref[0], ac_ref[0], ar_ref[0],
               state_ref.at[0], yf_ref, False, hp, ns)
    _ssd_chunk(xb_ref, bb_ref, cb_ref, dcb_ref[0, 0], drb_ref[0, 0], biasc_ref[1], biasr_ref[1], ac_ref[1], ar_ref[1],
               state_ref.at[1], yb_ref, True, hp, ns)

    @pl.when(c == pl.num_programs(1) - 1)
    def _():
        hout_ref[0] = state_ref[...]


def _ssd_scan(xs, bm, cm, dt_raw, dt_bias, a_log, h0, groups, hp):
    bsz, seq, inner = xs.shape
    heads = inner // hp
    nr = heads // groups
    ns = bm.shape[-1] // groups
    nc = seq // CHUNK
    gn = groups * ns
    dt4 = dt_raw.reshape(bsz, seq, 2, heads)
    dt_col = dt4.transpose(0, 2, 1, 3)
    dt_row = dt4.transpose(0, 2, 3, 1)
    bias = dt_bias.astype(F32).reshape(2, heads)
    a_neg = (-jnp.exp(a_log.astype(F32))).reshape(2, heads)
    h0t = h0.transpose(0, 1, 2, 4, 3)

    def fwd(w):
        return pl.BlockSpec((1, CHUNK, w), lambda b, c: (b, c, 0))

    def bwd(w):
        return pl.BlockSpec((1, CHUNK, w), lambda b, c: (b, nc - 1 - c, 0))

    vec_c = _const_spec((2, 1, heads))
    vec_r = _const_spec((2, heads, 1))
    state_spec = pl.BlockSpec((1, 2, heads, ns, hp), lambda b, c: (b, 0, 0, 0, 0))
    yf, yb, hout = pl.pallas_call(
        functools.partial(_ssd_scan_kernel, hp=hp, ns=ns),
        out_shape=(jax.ShapeDtypeStruct(xs.shape, BF16), jax.ShapeDtypeStruct(xs.shape, BF16),
                   jax.ShapeDtypeStruct(h0t.shape, F32)),
        grid=(bsz, nc),
        in_specs=[fwd(inner), fwd(gn), fwd(gn),
                  pl.BlockSpec((1, 1, CHUNK, heads), lambda b, c: (b, 0, c, 0)),
                  pl.BlockSpec((1, 1, heads, CHUNK), lambda b, c: (b, 0, 0, c)),
                  bwd(inner), bwd(gn), bwd(gn),
                  pl.BlockSpec((1, 1, CHUNK, heads), lambda b, c: (b, 1, nc - 1 - c, 0)),
                  pl.BlockSpec((1, 1, heads, CHUNK), lambda b, c: (b, 1, 0, nc - 1 - c)),
                  vec_c, vec_r, vec_c, vec_r, state_spec],
        out_specs=(fwd(inner), bwd(inner), state_spec),
        scratch_shapes=[pltpu.VMEM((2, heads, ns, hp), F32)],
        compiler_params=_cparams("parallel", "arbitrary"),
        name="ssd_scan",
    )(xs, bm, cm, dt_col, dt_row, xs, bm, cm, dt_col, dt_row,
      bias[:, None, :], bias[:, :, None], a_neg[:, None, :], a_neg[:, :, None], h0t)
    return yf, yb, hout.transpose(0, 1, 2, 4, 3)


def _ssd_out_prologue(rows, yf_ref, yb_ref, xs_ref, z_ref, d_ref, g_ref):
    y = yf_ref[0, rows].astype(F32) + yb_ref[0, rows].astype(F32) + xs_ref[0, rows].astype(F32) * d_ref[...]
    y = y * _silu(z_ref[0, rows].astype(F32))
    ms = jnp.mean(y * y, axis=-1, keepdims=True)
    return y * lax.rsqrt(ms + EPS) * g_ref[...]


def _diff_attn_kernel(*refs, hd, tk, has_ctx):
    if has_ctx:
        q_ref, k_ref, v_ref, ck_ref, cv_ref, lam_ref, gsub_ref, o_ref, vt_ref = refs
    else:
        q_ref, k_ref, v_ref, lam_ref, gsub_ref, o_ref, vt_ref = refs
    lk = k_ref.shape[1]
    blocks = [(k_ref, j * tk, j * tk, tk) for j in range(lk // tk)]
    if has_ctx:
        blocks.append((ck_ref, 0, lk, ck_ref.shape[1]))

    @pl.when(pl.program_id(2) == 0)
    def _():
        for j in range(lk // tk):
            vt_ref[:, j * tk:(j + 1) * tk] = v_ref[0, j * tk:(j + 1) * tk, :].astype(F32).T.astype(BF16)
        if has_ctx:
            vt_ref[:, lk:] = cv_ref[0].astype(F32).T.astype(BF16)

    q = q_ref[0].astype(F32)
    lane = lax.broadcasted_iota(jnp.int32, q.shape, 1)
    qz = [jnp.where(lane < hd, q, 0.0).astype(BF16), jnp.where(lane >= hd, q, 0.0).astype(BF16)]
    m, l, acc = [None, None], [None, None], [None, None]
    items = [(j, c) for j in range(len(blocks)) for c in range(2)]

    def scores(j, c):
        ref, r0, _, n = blocks[j]
        return _dot_nt(ref[0, r0:r0 + n, :], qz[c])

    nxt = scores(*items[0])
    for idx, (j, c) in enumerate(items):
        st = nxt
        if idx + 1 < len(items):
            nxt = scores(*items[idx + 1])
        _, _, c0, n = blocks[j]
        vtb = vt_ref[:, c0:c0 + n]
        mx = jnp.max(st, axis=0, keepdims=True)
        if j == 0:
            m[c] = mx
            p = jnp.exp2(st - mx)
            l[c] = jnp.sum(p, axis=0, keepdims=True)
            acc[c] = _dot(vtb, p.astype(BF16))
        else:
            m_new = jnp.maximum(m[c], mx)
            alpha = jnp.exp2(m[c] - m_new)
            p = jnp.exp2(st - m_new)
            l[c] = alpha * l[c] + jnp.sum(p, axis=0, keepdims=True)
            acc[c] = alpha * acc[c] + _dot(vtb, p.astype(BF16))
            m[c] = m_new
    lv = lam_ref[...]
    f1 = jnp.exp(jnp.sum(lv[0:1] * lv[1:2], axis=1, keepdims=True))
    f2 = jnp.exp(jnp.sum(lv[2:3] * lv[3:4], axis=1, keepdims=True))
    lam = f1 - f2 + DIFF_LAMBDA_INIT
    o = (acc[0] / l[0] - lam * (acc[1] / l[1])).T
    ms = jnp.mean(o * o, axis=-1, keepdims=True)
    o_ref[0] = (o * lax.rsqrt(ms + EPS) * gsub_ref[...] * (1.0 - DIFF_LAMBDA_INIT)).astype(o_ref.dtype)


def _diff_attn(q, k, v, ctx, lam_vecs, g_sub, heads, hd, tq, tk):
    bsz, lq, n = q.shape
    lk = k.shape[1]
    vd = 2 * hd
    kv_spec = pl.BlockSpec((1, lk, vd), lambda b, h, i: (b, 0, h))
    ins, specs, ltot = [q, k, v], [pl.BlockSpec((1, tq, vd), lambda b, h, i: (b, i, h)), kv_spec, kv_spec], lk
    if ctx is not None:
        lc = ctx[0].shape[1]
        ins += list(ctx)
        specs += [pl.BlockSpec((1, lc, vd), lambda b, h, i: (b, 0, h))] * 2
        ltot += lc
    return pl.pallas_call(
        functools.partial(_diff_attn_kernel, hd=hd, tk=tk, has_ctx=ctx is not None),
        out_shape=jax.ShapeDtypeStruct((bsz, lq, n), BF16),
        grid=(bsz, heads, lq // tq),
        in_specs=specs + [pl.BlockSpec(lam_vecs.shape, lambda b, h, i: (0, 0)),
                          pl.BlockSpec((1, vd), lambda b, h, i: (0, 0))],
        out_specs=pl.BlockSpec((1, tq, vd), lambda b, h, i: (b, i, h)),
        scratch_shapes=[pltpu.VMEM((vd, ltot), BF16)],
        compiler_params=_cparams("parallel", "parallel", "arbitrary"),
        name="diff_attn",
    )(*ins, lam_vecs, g_sub[None])


def _ml_proj_kernel(x_ref, xp_ref, xn_ref, a_ref, sh_ref, wm_ref, wo_ref, cw_ref, cb_ref, wq_ref, wk_ref, wv_ref,
                    wif_ref, bif_ref, q_ref, k_ref, v_ref, g_ref, op_ref, xe_ref, *, tm, nl, heads, ihd, dk):
    xe_ref[...] = _ext_rows(x_ref, xp_ref, xn_ref, a_ref[0], sh_ref[0], nl).astype(BF16)
    def in_proj(h):
        return _dot(xe_ref[...], wm_ref[:, h * ihd:(h + 1) * ihd])

    nxt = in_proj(0)
    op_ref[0] = _dot(xe_ref[HALO:HALO + tm, :], wo_ref[...]).astype(op_ref.dtype)
    gates = jnp.zeros((tm, bif_ref.shape[1]), F32) + bif_ref[...]
    for h in range(heads):
        sl = slice(h * ihd, (h + 1) * ihd)
        xm_e = nxt
        if h + 1 < heads:
            nxt = in_proj(h + 1)
        xc = _silu(_dwconv_rows(xm_e, cw_ref[:, sl], cb_ref[:, sl], 5))[HALO:HALO + tm].astype(BF16)
        xm = xm_e[HALO:HALO + tm].astype(BF16)
        q_ref[0, :, h * dk:(h + 1) * dk] = _dot(xc, wq_ref[h]).astype(q_ref.dtype)
        k_ref[0, :, h * dk:(h + 1) * dk] = (_dot(xc, wk_ref[h]) * (dk ** -0.5)).astype(k_ref.dtype)
        v_ref[0, :, sl] = _dot(xm, wv_ref[h]).astype(v_ref.dtype)
        gates = gates + _dot(xc, wif_ref[sl, :])
    g_ref[0] = gates


def _ml_project(x, a, sh, w_in, conv_w, conv_b, w_q, w_k, w_v, w_if, b_if, tm):
    bsz, seq, d = x.shape
    heads, ihd, dk = w_q.shape
    inner = heads * ihd
    nl = seq // tm
    ng = w_if.shape[1]
    wm = w_in[:, :inner].astype(BF16)
    wo = w_in[:, inner:].astype(BF16)
    mod_spec = pl.BlockSpec((1, 1, d), lambda b, l: (b, 0, 0))
    sds = lambda n, dt: jax.ShapeDtypeStruct((bsz, seq, n), dt)
    return pl.pallas_call(
        functools.partial(_ml_proj_kernel, tm=tm, nl=nl, heads=heads, ihd=ihd, dk=dk),
        out_shape=(sds(heads * dk, BF16), sds(heads * dk, BF16), sds(inner, BF16), sds(ng, F32), sds(inner, BF16)),
        grid=(bsz, nl),
        in_specs=_halo_specs(tm, d, nl) + [mod_spec, mod_spec, _const_spec(wm.shape), _const_spec(wo.shape),
                                           _const_spec(conv_w.shape), _const_spec((1, inner)),
                                           _const_spec(w_q.shape), _const_spec(w_k.shape), _const_spec(w_v.shape),
                                           _const_spec(w_if.shape), _const_spec((1, ng))],
        out_specs=(_row_spec(tm, heads * dk), _row_spec(tm, heads * dk), _row_spec(tm, inner), _row_spec(tm, ng),
                   _row_spec(tm, inner)),
        scratch_shapes=[pltpu.VMEM((tm + 2 * HALO, d), BF16)],
        compiler_params=_cparams("parallel", "parallel"),
        name="ml_proj",
    )(x, x, x, a, sh, wm, wo, conv_w, conv_b[None], w_q.astype(BF16), w_k.astype(BF16), w_v.astype(BF16),
      w_if.astype(BF16), b_if.reshape(1, ng))


def _ml_gate_free(q_ref, k_ref, c_ref):
    heads, dk, _ = c_ref.shape
    qk, qc = [], []
    for hh in range(heads):
        q = q_ref[0, :, hh * dk:(hh + 1) * dk]
        qk.append(_dot_nt(q, k_ref[0, :, hh * dk:(hh + 1) * dk]))
        qc.append(_dot(q, c_ref[hh].astype(BF16)))
    return qk, qc


def _ml_gate_cumsums(gc, gr, reverse):
    _, t_mat, t_mat_t = _tri_masks(reverse)
    return _cumsum_cols(t_mat.astype(BF16), _log_sigmoid(gc)), _dot_exact(_log_sigmoid(gr), t_mat_t)


def _ml_chunk(q_ref, k_ref, v_ref, gc, gr, d, c_ref, n_ref, m_ref, h_ref, reverse, bcum_call, bcum_rall, qk, qc):
    heads, dk, dv = c_ref.shape
    tri = _tri_masks(reverse)[0]
    end = 0 if reverse else CHUNK - 1
    for hh in range(heads):
        ci, cf = d * 2 * heads + hh, d * 2 * heads + heads + hh
        q = q_ref[0, :, hh * dk:(hh + 1) * dk]
        k = k_ref[0, :, hh * dk:(hh + 1) * dk]
        v = v_ref[0, :, hh * dv:(hh + 1) * dv]
        bcum_c = bcum_call[:, cf:cf + 1]
        bcum_r = bcum_rall[cf:cf + 1, :]
        i_c = gc[:, ci:ci + 1]
        i_r = gr[ci:ci + 1, :]
        m_old = m_ref[hh]
        cmat = c_ref[hh]
        nrow = n_ref[hh]
        dmat = jnp.where(tri, bcum_c - bcum_r + i_r, NEG_INF)
        inter = bcum_c + m_old
        mt = jnp.maximum(jnp.max(dmat, axis=1, keepdims=True), inter)
        s = qk[hh] * jnp.exp(dmat - mt)
        w_int = jnp.exp(inter - mt)
        num = _dot(s.astype(BF16), v) + w_int * qc[hh]
        den = jnp.sum(s, axis=1, keepdims=True) + w_int * jnp.sum(q.astype(F32) * nrow, axis=1, keepdims=True)
        h_ref[0, :, hh * dv:(hh + 1) * dv] = (num / jnp.maximum(jnp.abs(den), jnp.exp(-mt))).astype(h_ref.dtype)
        btot = bcum_c[end:end + 1]
        g = btot - bcum_c + i_c
        m_new = jnp.maximum(btot + m_old, jnp.max(g, axis=0, keepdims=True))
        decay = jnp.exp(btot + m_old - m_new)
        kw = k.astype(F32) * jnp.exp(g - m_new)
        c_ref[hh] = cmat * decay + _dot(kw.T.astype(BF16), v)
        n_ref[hh] = nrow * decay + jnp.sum(kw, axis=0, keepdims=True)
        m_ref[hh] = m_new


def _ml_scan_kernel(qf_ref, kf_ref, vf_ref, gcf_ref, grf_ref, qb_ref, kb_ref, vb_ref, gcb_ref, grb_ref,
                    c0_ref, n0_ref, m0_ref, hf_ref, hb_ref, cout_ref, nout_ref, mout_ref, c_ref, n_ref, m_ref):
    c = pl.program_id(1)

    @pl.when(c == 0)
    def _():
        c_ref[...] = c0_ref[0]
        n_ref[...] = n0_ref[0]
        m_ref[...] = m0_ref[0]

    pre_f = _ml_gate_free(qf_ref, kf_ref, c_ref.at[0])
    pre_b = _ml_gate_free(qb_ref, kb_ref, c_ref.at[1])
    cum_f = _ml_gate_cumsums(gcf_ref[0], grf_ref[0], False)
    _ml_chunk(qf_ref, kf_ref, vf_ref, gcf_ref[0], grf_ref[0], 0, c_ref.at[0], n_ref.at[0], m_ref.at[0], hf_ref, False,
              *cum_f, *pre_f)
    cum_b = _ml_gate_cumsums(gcb_ref[0], grb_ref[0], True)
    _ml_chunk(qb_ref, kb_ref, vb_ref, gcb_ref[0], grb_ref[0], 1, c_ref.at[1], n_ref.at[1], m_ref.at[1], hb_ref, True,
              *cum_b, *pre_b)

    @pl.when(c == pl.num_programs(1) - 1)
    def _():
        cout_ref[0] = c_ref[...]
        nout_ref[0] = n_ref[...]
        mout_ref[0] = m_ref[...]


def _ml_scan(q, k, v, gates, c0, n0, m0):
    bsz, seq, _ = q.shape
    _, _, heads, dk, dv = c0.shape
    ng = gates.shape[-1]
    nc = seq // CHUNK
    g_row = gates.transpose(0, 2, 1)
    n0 = n0.reshape(bsz, 2, heads, 1, dk)
    m0 = m0.reshape(bsz, 2, heads, 1, 1)

    def fwd(w):
        return pl.BlockSpec((1, CHUNK, w), lambda b, c: (b, c, 0))

    def bwd(w):
        return pl.BlockSpec((1, CHUNK, w), lambda b, c: (b, nc - 1 - c, 0))

    c_spec = pl.BlockSpec((1, 2, heads, dk, dv), lambda b, c: (b, 0, 0, 0, 0))
    n_spec = pl.BlockSpec((1, 2, heads, 1, dk), lambda b, c: (b, 0, 0, 0, 0))
    m_spec = pl.BlockSpec((1, 2, heads, 1, 1), lambda b, c: (b, 0, 0, 0, 0))
    hf, hb, cout, nout, mout = pl.pallas_call(
        _ml_scan_kernel,
        out_shape=(jax.ShapeDtypeStruct(v.shape, BF16), jax.ShapeDtypeStruct(v.shape, BF16),
                   jax.ShapeDtypeStruct(c0.shape, F32), jax.ShapeDtypeStruct(n0.shape, F32),
                   jax.ShapeDtypeStruct(m0.shape, F32)),
        grid=(bsz, nc),
        in_specs=[fwd(heads * dk), fwd(heads * dk), fwd(heads * dv), fwd(ng),
                  pl.BlockSpec((1, ng, CHUNK), lambda b, c: (b, 0, c)),
                  bwd(heads * dk), bwd(heads * dk), bwd(heads * dv), bwd(ng),
                  pl.BlockSpec((1, ng, CHUNK), lambda b, c: (b, 0, nc - 1 - c)),
                  c_spec, n_spec, m_spec],
        out_specs=(fwd(heads * dv), bwd(heads * dv), c_spec, n_spec, m_spec),
        scratch_shapes=[pltpu.VMEM((2, heads, dk, dv), F32), pltpu.VMEM((2, heads, 1, dk), F32),
                        pltpu.VMEM((2, heads, 1, 1), F32)],
        compiler_params=_cparams("parallel", "arbitrary"),
        name="ml_scan",
    )(q, k, v, gates, g_row, q, k, v, gates, g_row, c0, n0, m0)
    return hf, hb, cout, nout.reshape(bsz, 2, heads, dk), mout.reshape(bsz, 2, heads)


def _ml_out_prologue(rows, hf_ref, hb_ref, op_ref, g_ref, *, heads, dv):
    h = hf_ref[0, rows].astype(F32) + hb_ref[0, rows].astype(F32)
    outs = []
    for hh in range(heads):
        blk = h[:, hh * dv:(hh + 1) * dv]
        ms = jnp.mean(blk * blk, axis=-1, keepdims=True)
        outs.append(blk * lax.rsqrt(ms + EPS))
    return jax.nn.sigmoid(op_ref[0, rows].astype(F32)) * (jnp.concatenate(outs, axis=1) * g_ref[...])


def _stream_mods(mod, g1, g2):
    a1 = (g1 * (1.0 + mod[:, 1]))[:, None]
    a2 = (g2 * (1.0 + mod[:, 4]))[:, None]
    return (a1, mod[:, 0][:, None], mod[:, 2][:, None]), (a2, mod[:, 3][:, None], mod[:, 5][:, None])


def _tile(seq):
    return min(seq, 512)


def _attn_layer(x, mods, lat_cache, w_qkv, g_q, g_k, sink, w_o):
    a, sh, gt = mods
    bsz, seq, d = x.shape
    hd = g_q.shape[0]
    heads = sink.shape[0]
    kvh = (w_qkv.shape[1] // hd - heads) // 2
    tm = _tile(seq)
    gq = jnp.tile(g_q, LANES // hd)[None]
    gk = jnp.tile(g_k, LANES // hd)[None]
    qscale = hd ** -0.5 * LOG2E
    if lat_cache is None:
        q, k, v = _qkv_project(x, a, sh, w_qkv, gq, gk, heads * hd, kvh * hd, kvh * hd, None, F32, tm, qscale)
        o = _attn_ctx(q, k, v, sink, kvh, heads // kvh, hd)
    else:
        ck, cv = lat_cache
        q, k, v = _qkv_project(x, a, sh, w_qkv, gq, gk, heads * hd, kvh * hd, kvh * hd, _rope_tables(seq, hd), BF16, tm,
                               qscale)
        o = _attn_lat(q, k, v, ck.reshape(bsz, -1, kvh * hd), cv.reshape(bsz, -1, kvh * hd), sink, kvh,
                      heads // kvh, hd, 128)
    x = _oproj(_rows_prologue, [o],[_row_spec(tm, heads * hd)], x, gt, w_o, tm)
    return x, k.reshape(bsz, seq, kvh, hd), v.reshape(bsz, seq, kvh, hd)


def _ssd_layer(x, mods, h0, w_in, conv_w, conv_b, dt_bias, a_log, d_skip, g_norm, w_out):
    a, sh, gt = mods
    bsz, seq, d = x.shape
    heads = d_skip.shape[0]
    inner = g_norm.shape[0]
    hp = inner // heads
    gs = (conv_w.shape[1] - inner) // 2
    groups = gs // LANES
    tm = _tile(seq)
    z, xs, bm, cm, dt_raw = _ssd_project(x, a, sh, w_in, conv_w, conv_b, inner, gs, 2 * heads, tm)
    yf, yb, hout = _ssd_scan(xs, bm, cm, dt_raw, dt_bias, a_log, h0, groups, hp)
    rs = _row_spec(tm, inner)
    x = _oproj(_ssd_out_prologue, [yf, yb, xs, z, jnp.repeat(d_skip, hp)[None], g_norm[None]],
               [rs, rs, rs, rs, _const_spec((1, inner)), _const_spec((1, inner))], x, gt, w_out, tm)
    return x, hout


def _diff_layer(x, mods, lat_cache, w_qkv, g_q, g_k, lam_vecs, g_sub, w_o):
    a, sh, gt = mods
    bsz, seq, d = x.shape
    hd = g_q.shape[1]
    vd = g_sub.shape[0]
    heads = w_qkv.shape[1] // (4 * hd + vd)
    nq = heads * 2 * hd
    tm = _tile(seq)
    gq = g_q.reshape(1, 2 * hd)
    gk = g_k.reshape(1, 2 * hd)
    qscale = hd ** -0.5 * math.log2(math.e)
    if lat_cache is None:
        q, k, v = _qkv_project(x, a, sh, w_qkv, gq, gk, nq, nq, heads * vd, None, F32, tm, qscale)
        o = _diff_attn(q, k.astype(BF16), v.astype(BF16), None, lam_vecs, g_sub, heads, hd, seq, seq)
    else:
        ck, cv = lat_cache
        q, k, v = _qkv_project(x, a, sh, w_qkv, gq, gk, nq, nq, heads * vd, _rope_tables(seq, hd), BF16, tm, qscale)
        ctx = (ck.reshape(bsz, -1, nq).astype(BF16), cv.reshape(bsz, -1, heads * vd).astype(BF16))
        o = _diff_attn(q, k, v, ctx, lam_vecs, g_sub, heads, hd, tm, min(seq, 1024))
    x = _oproj(_rows_prologue, [o],[_row_spec(tm, heads * vd)], x, gt, w_o, tm)
    return x, k.reshape(bsz, seq, heads, 2, hd), v.reshape(bsz, seq, heads, vd)


def _ml_layer(x, mods, c0, n0, m0, w_in, conv_w, conv_b, w_q, w_k, w_v, w_if, b_if, g_norm, w_out):
    a, sh, gt = mods
    bsz, seq, d = x.shape
    heads, dv = g_norm.shape
    inner = heads * dv
    tm = _tile(seq)
    q, k, v, gates, o_pre = _ml_project(x, a, sh, w_in, conv_w, conv_b, w_q, w_k, w_v, w_if, b_if, tm)
    hf, hb, cout, nout, mout = _ml_scan(q, k, v, gates, c0, n0, m0)
    rs = _row_spec(tm, inner)
    x = _oproj(functools.partial(_ml_out_prologue, heads=heads, dv=dv), [hf, hb, o_pre, g_norm.reshape(1, inner)],
               [rs, rs, rs, _const_spec((1, inner))], x, gt, w_out, tm)
    return x, cout, nout, mout


def kernel(x_prompt, x_sample, cache_attn_k, cache_attn_v, state_ssd, cache_diff_k, cache_diff_v, state_mlstm_c, state_mlstm_n, state_mlstm_m, c, c_ctx, ada_w, ada_b, norm1_g, norm2_g, ffn_w_up, ffn_conv_w, ffn_conv_b, ffn_w_down, attn_w_qkv, attn_g_q, attn_g_k, attn_sink, attn_w_o, ssd_w_in, ssd_conv_w, ssd_conv_b, ssd_dt_bias, ssd_a_log, ssd_d, ssd_g_norm, ssd_w_out, diff_w_qkv, diff_g_q, diff_g_k, diff_lq1, diff_lk1, diff_lq2, diff_lk2, diff_g_sub, diff_w_o, ml_w_in, ml_conv_w, ml_conv_b, ml_w_q, ml_w_k, ml_w_v, ml_w_if, ml_b_if, ml_g_norm, ml_w_out):
    xp, xs = x_prompt, x_sample
    bp, bs = xp.shape[0], xs.shape[0]
    d = xp.shape[-1]
    depth = ada_w.shape[0]
    rows = 16
    cond = jnp.concatenate([c_ctx[None], c, jnp.zeros((rows - 1 - bs, d), F32)], axis=0)
    mod = _ada(cond, ada_w, ada_b).reshape(depth, rows, 6, d)
    lam_vecs = jnp.stack([diff_lq1, diff_lk1, diff_lq2, diff_lk2])
    outs = {}
    for i in range(depth):
        mp1, mp2 = _stream_mods(jnp.broadcast_to(mod[i, 0], (bp, 6, d)), norm1_g[i], norm2_g[i])
        ms1, ms2 = _stream_mods(mod[i, 1:1 + bs], norm1_g[i], norm2_g[i])
        kind = i % 4
        if kind == 0:
            xp, outs["ak"], outs["av"] = _attn_layer(xp, mp1, None, attn_w_qkv, attn_g_q, attn_g_k, attn_sink, attn_w_o)
            xs, _, _ = _attn_layer(xs, ms1, (cache_attn_k, cache_attn_v), attn_w_qkv, attn_g_q, attn_g_k, attn_sink,
                                   attn_w_o)
        elif kind == 1:
            ssd_w = (ssd_w_in, ssd_conv_w, ssd_conv_b, ssd_dt_bias, ssd_a_log, ssd_d, ssd_g_norm, ssd_w_out)
            xp, outs["ssd"] = _ssd_layer(xp, mp1, jnp.zeros((bp,) + state_ssd.shape[1:], F32), *ssd_w)
            xs, _ = _ssd_layer(xs, ms1, state_ssd, *ssd_w)
        elif kind == 2:
            diff_w = (diff_w_qkv, diff_g_q, diff_g_k, lam_vecs, diff_g_sub, diff_w_o)
            xp, outs["dk"], outs["dv"] = _diff_layer(xp, mp1, None, *diff_w)
            xs, _, _ = _diff_layer(xs, ms1, (cache_diff_k, cache_diff_v), *diff_w)
        else:
            ml_w = (ml_w_in, ml_conv_w, ml_conv_b, ml_w_q, ml_w_k, ml_w_v, ml_w_if, ml_b_if, ml_g_norm, ml_w_out)
            zc = jnp.zeros((bp,) + state_mlstm_c.shape[1:], F32)
            zn = jnp.zeros((bp,) + state_mlstm_n.shape[1:], F32)
            zm = jnp.zeros((bp,) + state_mlstm_m.shape[1:], F32)
            xp, outs["mc"], outs["mn"], outs["mm"] = _ml_layer(xp, mp1, zc, zn, zm, *ml_w)
            xs, _, _, _ = _ml_layer(xs, ms1, state_mlstm_c, state_mlstm_n, state_mlstm_m, *ml_w)
        ffn_w = (ffn_w_up[i], ffn_conv_w[i], ffn_conv_b[i], ffn_w_down[i])
        xp = _ffn(xp, *mp2, *ffn_w, tm=_tile(xp.shape[1]))
        xs = _ffn(xs, *ms2, *ffn_w, tm=_tile(xs.shape[1]))
    return (xp, xs, outs["ak"], outs["av"], outs["ssd"], outs["dk"], outs["dv"], outs["mc"], outs["mn"], outs["mm"])
```

```python
import functools
import math

import jax
import jax.numpy as jnp
from jax import lax
from jax.experimental import pallas as pl
from jax.experimental.pallas import tpu as pltpu

F32 = jnp.float32
BF16 = jnp.bfloat16
HIGHEST = lax.Precision.HIGHEST

EPS = 1e-6
ROPE_THETA = 10000.0
GRID_W = 64
DIFF_LAMBDA_INIT = 0.8 - 0.6 * math.exp(-0.3 * 2)

VMEM_LIMIT_BYTES = 56 * 1024 * 1024
LANES = 128
MXU_ROWS = 256
BF16_SUBLANES = 16
HALO = BF16_SUBLANES
CHUNK = 128
NEG_INF = float("-inf")
LOG2E = math.log2(math.e)


def _cparams(*sem):
    return pltpu.CompilerParams(dimension_semantics=sem, vmem_limit_bytes=VMEM_LIMIT_BYTES)


def _dot(a, b):
    return jnp.dot(a, b, preferred_element_type=F32)


def _dot_nt(a, b):
    return lax.dot_general(a, b, (((1,), (1,)), ((), ())), preferred_element_type=F32)


def _dot_exact(a, b):
    return jnp.dot(a, b, preferred_element_type=F32, precision=HIGHEST)


def _norm_mod(x, a, sh):
    ms = jnp.mean(x * x, axis=-1, keepdims=True)
    return x * lax.rsqrt(ms + EPS) * a + sh


def _silu(x):
    return x * jax.nn.sigmoid(x)


def _softplus(x):
    return jnp.maximum(x, 0.0) + jnp.log(1.0 + jnp.exp(-jnp.abs(x)))


def _log_sigmoid(x):
    return jnp.minimum(x, 0.0) - jnp.log(1.0 + jnp.exp(-jnp.abs(x)))


def _const_spec(shape):
    nd = len(shape)
    return pl.BlockSpec(shape, lambda *_: (0,) * nd)


def _ext_rows(x_ref, xp_ref, xn_ref, a, sh, nl):
    l = pl.program_id(1)
    xt = _norm_mod(x_ref[0], a, sh)
    xp = _norm_mod(xp_ref[0], a, sh) * (l > 0).astype(F32)
    xn = _norm_mod(xn_ref[0], a, sh) * (l < nl - 1).astype(F32)
    return jnp.concatenate([xp, xt, xn], axis=0)


def _halo_specs(tm, d, nl):
    r = tm // HALO
    return [
        pl.BlockSpec((1, tm, d), lambda b, l: (b, l, 0)),
        pl.BlockSpec((1, HALO, d), lambda b, l: (b, jnp.maximum(l * r - 1, 0), 0)),
        pl.BlockSpec((1, HALO, d), lambda b, l: (b, jnp.minimum((l + 1) * r, nl * r - 1), 0)),
    ]


def _row_blocks(tm):
    n = max(tm // MXU_ROWS, 1)
    return [slice(i * (tm // n), (i + 1) * (tm // n)) for i in range(n)]


def _dwconv_rows(h, w, b, width):
    rows = h.shape[0]
    pad = width // 2
    y = h * w[pad:pad + 1]
    for k in range(width):
        if k == pad:
            continue
        y = y + pltpu.roll(h, (pad - k) % rows, 0) * w[k:k + 1]
    return y + b


def _ada_kernel(c_ref, w_ref, b_ref, o_ref):
    s = _silu(c_ref[...]).astype(BF16)
    o_ref[0] = _dot(s, w_ref[0].astype(BF16)) + b_ref[0]


def _ada(cond, ada_w, ada_b):
    depth, d, n = ada_w.shape
    r = cond.shape[0]
    tn = 1536
    return pl.pallas_call(
        _ada_kernel,
        out_shape=jax.ShapeDtypeStruct((depth, r, n), F32),
        grid=(depth, n // tn),
        in_specs=[
            pl.BlockSpec((r, d), lambda i, j: (0, 0)),
            pl.BlockSpec((1, d, tn), lambda i, j: (i, 0, j)),
            pl.BlockSpec((1, 1, tn), lambda i, j: (i, 0, j)),
        ],
        out_specs=pl.BlockSpec((1, r, tn), lambda i, j: (i, 0, j)),
        compiler_params=_cparams("parallel", "parallel"),
        name="ada",
    )(cond, ada_w, ada_b.reshape(depth, 1, n))


def _ffn_kernel(x_ref, xp_ref, xn_ref, a_ref, sh_ref, gt_ref, wu_ref, cw_ref, cb_ref, wd_ref, o_ref, xe_ref, acc_ref,
                *, tm, nl, nchunk):
    xe_ref[...] = _ext_rows(x_ref, xp_ref, xn_ref, a_ref[0], sh_ref[0], nl).astype(BF16)

    dff = wd_ref.shape[0]
    cn = dff // nchunk

    def cols(j, half):
        return slice(half * dff + j * cn, half * dff + (j + 1) * cn)

    def up(j):
        xe = xe_ref[...]
        return _dot(xe, wu_ref[:, cols(j, 0)]), _dot(xe, wu_ref[:, cols(j, 1)])

    nxt = up(0)
    for j in range(nchunk):
        ua, ug = nxt
        if j + 1 < nchunk:
            nxt = up(j + 1)
        ha = _dwconv_rows(ua, cw_ref[:, cols(j, 0)], cb_ref[:, cols(j, 0)], 3)[HALO:HALO + tm]
        hg = _dwconv_rows(ug, cw_ref[:, cols(j, 1)], cb_ref[:, cols(j, 1)], 3)[HALO:HALO + tm]
        part = _dot((ha * _silu(hg)).astype(BF16), wd_ref[j * cn:(j + 1) * cn, :])
        if j == 0:
            acc_ref[...] = part
        else:
            acc_ref[...] += part
    o_ref[0] = x_ref[0] + gt_ref[0] * acc_ref[...]


def _ffn(x, a, sh, gt, w_up, conv_w, conv_b, w_down, tm):
    bsz, seq, d = x.shape
    dff = w_down.shape[0]
    nchunk = dff // MXU_ROWS
    nl = seq // tm
    mod_spec = pl.BlockSpec((1, 1, d), lambda b, l: (b, 0, 0))
    return pl.pallas_call(
        functools.partial(_ffn_kernel, tm=tm, nl=nl, nchunk=nchunk),
        out_shape=jax.ShapeDtypeStruct(x.shape, F32),
        grid=(bsz, nl),
        in_specs=_halo_specs(tm, d, nl) + [mod_spec, mod_spec, mod_spec, _const_spec(w_up.shape),
                                           _const_spec(conv_w.shape), _const_spec((1, 2 * dff)),
                                           _const_spec(w_down.shape)],
        out_specs=pl.BlockSpec((1, tm, d), lambda b, l: (b, l, 0)),
        scratch_shapes=[pltpu.VMEM((tm + 2 * HALO, d), BF16), pltpu.VMEM((tm, d), F32)],
        compiler_params=_cparams("parallel", "parallel"),
        name="ffn",
    )(x, x, x, a, sh, gt, w_up.astype(BF16), conv_w, conv_b[None], w_down.astype(BF16))


def _oproj_kernel(*refs, prologue, n_in):
    in_refs = refs[:n_in]
    x_ref, gt_ref, w_ref, o_ref = refs[n_in:]
    for rows in _row_blocks(x_ref.shape[1]):
        lhs = prologue(rows, *in_refs)
        o_ref[0, rows] = x_ref[0, rows] + gt_ref[0] * _dot(lhs.astype(BF16), w_ref[...])


def _oproj(prologue, ins, in_specs, x, gt, w, tm):
    bsz, seq, d = x.shape
    return pl.pallas_call(
        functools.partial(_oproj_kernel, prologue=prologue, n_in=len(ins)),
        out_shape=jax.ShapeDtypeStruct(x.shape, F32),
        grid=(bsz, seq // tm),
        in_specs=list(in_specs) + [pl.BlockSpec((1, tm, d), lambda b, l: (b, l, 0)),
                                   pl.BlockSpec((1, 1, d), lambda b, l: (b, 0, 0)),
                                   _const_spec(w.shape)],
        out_specs=pl.BlockSpec((1, tm, d), lambda b, l: (b, l, 0)),
        compiler_params=_cparams("parallel", "parallel"),
        name="oproj",
    )(*ins, x, gt, w.astype(BF16))


def _rows_prologue(rows, o_ref):
    return o_ref[0, rows]


def _row_spec(tm, n):
    return pl.BlockSpec((1, tm, n), lambda b, l: (b, l, 0))


def _head_sumsq_matrix(hd):
    i = lax.broadcasted_iota(jnp.int32, (LANES, LANES), 0) // hd
    j = lax.broadcasted_iota(jnp.int32, (LANES, LANES), 1) // hd
    return jnp.where(i == j, 1.0 / hd, 0.0).astype(BF16)


def _qk_norm_rope(y, gain, rope, avg):
    outs = []
    for c in range(y.shape[1] // LANES):
        yb = y[:, c * LANES:(c + 1) * LANES]
        ms = _dot((yb * yb).astype(BF16), avg)
        yb = yb * lax.rsqrt(ms + EPS) * gain
        if rope is not None:
            cos, s_lo, s_hi = rope
            yb = yb * cos + pltpu.roll(yb, LANES - 32, 1) * s_lo + pltpu.roll(yb, 32, 1) * s_hi
        outs.append(yb)
    return outs


def _rope_tables(seq, hd):
    rows = seq // GRID_W
    row = jnp.repeat(jnp.arange(rows, dtype=F32), GRID_W)
    col = jnp.tile(jnp.arange(GRID_W, dtype=F32), rows)
    nf = hd // 4
    inv = ROPE_THETA ** (-jnp.arange(nf, dtype=F32) / nf)
    ang = jnp.concatenate([row[:, None] * inv, col[:, None] * inv], axis=-1)
    cos, sin = jnp.cos(ang), jnp.sin(ang)
    zero = jnp.zeros_like(sin)
    rep = LANES // hd
    cos_t = jnp.tile(jnp.concatenate([cos, cos], axis=-1), (1, rep))
    s_lo = jnp.tile(jnp.concatenate([-sin, zero], axis=-1), (1, rep))
    s_hi = jnp.tile(jnp.concatenate([zero, sin], axis=-1), (1, rep))
    return cos_t, s_lo, s_hi


def _attn_qkv_kernel(*refs, use_rope, nq, nk, qscale):
    if use_rope:
        x_ref, a_ref, sh_ref, w_ref, gq_ref, gk_ref, cos_ref, slo_ref, shi_ref, q_ref, k_ref, v_ref = refs
        rope = (cos_ref, slo_ref, shi_ref)
    else:
        x_ref, a_ref, sh_ref, w_ref, gq_ref, gk_ref, q_ref, k_ref, v_ref = refs
        rope = None
    avg = _head_sumsq_matrix(64)
    for rows in _row_blocks(x_ref.shape[1]):
        xn = _norm_mod(x_ref[0, rows], a_ref[0], sh_ref[0]).astype(BF16)
        y = _dot(xn, w_ref[...])
        rope_rows = None if rope is None else tuple(t[rows] for t in rope)
        q = _qk_norm_rope(y[:, :nq], gq_ref[...], rope_rows, avg)
        k = _qk_norm_rope(y[:, nq:nq + nk], gk_ref[...], rope_rows, avg)
        for c, blk in enumerate(q):
            q_ref[0, rows, c * LANES:(c + 1) * LANES] = (blk * qscale).astype(q_ref.dtype)
        for c, blk in enumerate(k):
            k_ref[0, rows, c * LANES:(c + 1) * LANES] = blk.astype(k_ref.dtype)
        v_ref[0, rows] = y[:, nq + nk:].astype(v_ref.dtype)


def _qkv_project(x, a, sh, w, gq, gk, nq, nk, nv, rope, kv_dtype, tm, qscale):
    bsz, seq, d = x.shape
    mod_spec = pl.BlockSpec((1, 1, d), lambda b, l: (b, 0, 0))
    ins = [x, a, sh, w.astype(BF16), gq, gk]
    specs = [_row_spec(tm, d), mod_spec, mod_spec, _const_spec(w.shape), _const_spec(gq.shape), _const_spec(gk.shape)]
    if rope is not None:
        ins += list(rope)
        specs += [pl.BlockSpec((tm, LANES), lambda b, l: (l, 0))] * 3
    return pl.pallas_call(
        functools.partial(_attn_qkv_kernel, use_rope=rope is not None, nq=nq, nk=nk, qscale=qscale),
        out_shape=(jax.ShapeDtypeStruct((bsz, seq, nq), BF16),
                   jax.ShapeDtypeStruct((bsz, seq, nk), kv_dtype),
                   jax.ShapeDtypeStruct((bsz, seq, nv), kv_dtype)),
        grid=(bsz, seq // tm),
        in_specs=specs,
        out_specs=(_row_spec(tm, nq), _row_spec(tm, nk), _row_spec(tm, nv)),
        compiler_params=_cparams("parallel", "parallel"),
        name="qkv",
    )(*ins)


def _sink_softmax_pv(parts, sink2):
    m = sink2
    for s, _ in parts:
        m = jnp.maximum(m, jnp.max(s, axis=1, keepdims=True))
    den = jnp.exp2(sink2 - m)
    acc = None
    for s, v in parts:
        e = jnp.exp2(s - m)
        den = den + jnp.sum(e, axis=1, keepdims=True)
        pv = _dot(e.astype(BF16), v)
        acc = pv if acc is None else acc + pv
    return acc / den


def _attn_ctx_kernel(sink_ref, q_ref, k_ref, v_ref, o_ref, *, kvh, group, hd):
    for g in range(kvh):
        kh = k_ref[0, :, g * hd:(g + 1) * hd].astype(BF16)
        vh = v_ref[0, :, g * hd:(g + 1) * hd].astype(BF16)
        for r in range(group):
            h = g * group + r
            s = _dot_nt(q_ref[0, :, h * hd:(h + 1) * hd], kh)
            o_ref[0, :, h * hd:(h + 1) * hd] = _sink_softmax_pv([(s, vh)], sink_ref[h] * LOG2E).astype(o_ref.dtype)


def _attn_ctx(q, k, v, sink, kvh, group, hd):
    bsz, seq, nq = q.shape
    nk = k.shape[-1]
    return pl.pallas_call(
        functools.partial(_attn_ctx_kernel, kvh=kvh, group=group, hd=hd),
        out_shape=jax.ShapeDtypeStruct((bsz, seq, nq), BF16),
        grid=(bsz,),
        in_specs=[pl.BlockSpec(memory_space=pltpu.SMEM),
                  pl.BlockSpec((1, seq, nq), lambda b: (b, 0, 0)),
                  pl.BlockSpec((1, seq, nk), lambda b: (b, 0, 0)),
                  pl.BlockSpec((1, seq, nk), lambda b: (b, 0, 0))],
        out_specs=pl.BlockSpec((1, seq, nq), lambda b: (b, 0, 0)),
        compiler_params=_cparams("parallel"),
        name="attn_ctx",
    )(sink, q, k, v)


def _attn_lat_kernel(sink_ref, q_ref, kp_ref, ko_ref, kn_ref, vp_ref, vo_ref, vn_ref, ck_ref, cv_ref, o_ref, *,
                     kvh, group, hd, blk):
    n = pl.program_id(1)
    nb = pl.num_programs(1)
    qi = lax.broadcasted_iota(jnp.int32, (blk, 3 * blk), 0)
    kj = lax.broadcasted_iota(jnp.int32, (blk, 3 * blk), 1)
    ok = (kj >= qi) & (kj <= qi + 2 * blk)
    ok = ok & ((kj >= blk) | (n > 0)) & ((kj < 2 * blk) | (n < nb - 1))
    bias = jnp.where(ok, 0.0, NEG_INF)
    def keys_values(g):
        sl = slice(g * hd, (g + 1) * hd)
        k_loc = jnp.concatenate([kp_ref[0, :, sl], ko_ref[0, :, sl], kn_ref[0, :, sl]], axis=0)
        v_loc = jnp.concatenate([vp_ref[0, :, sl], vo_ref[0, :, sl], vn_ref[0, :, sl]], axis=0)
        return k_loc, ck_ref[0, :, sl].astype(BF16), v_loc, cv_ref[0, :, sl].astype(BF16)

    def scores(h, kv):
        qh = q_ref[0, :, h * hd:(h + 1) * hd]
        return _dot_nt(qh, kv[0]) + bias, _dot_nt(qh, kv[1])

    kv = keys_values(0)
    nxt = scores(0, kv)
    for h in range(kvh * group):
        s_loc, s_ctx = nxt
        v_loc, v_ctx = kv[2], kv[3]
        if h + 1 < kvh * group:
            if (h + 1) % group == 0:
                kv = keys_values((h + 1) // group)
            nxt = scores(h + 1, kv)
        o = _sink_softmax_pv([(s_loc, v_loc), (s_ctx, v_ctx)], sink_ref[h] * LOG2E)
        o_ref[0, :, h * hd:(h + 1) * hd] = o.astype(o_ref.dtype)


def _attn_lat(q, k, v, ck, cv, sink, kvh, group, hd, blk):
    bsz, seq, nq = q.shape
    nk = k.shape[-1]
    nctx = ck.shape[1]
    nb = seq // blk
    prev = pl.BlockSpec((1, blk, nk), lambda b, n: (b, jnp.maximum(n - 1, 0), 0))
    own = pl.BlockSpec((1, blk, nk), lambda b, n: (b, n, 0))
    nxt = pl.BlockSpec((1, blk, nk), lambda b, n: (b, jnp.minimum(n + 1, nb - 1), 0))
    ctx = pl.BlockSpec((1, nctx, nk), lambda b, n: (b, 0, 0))
    return pl.pallas_call(
        functools.partial(_attn_lat_kernel, kvh=kvh, group=group, hd=hd, blk=blk),
        out_shape=jax.ShapeDtypeStruct((bsz, seq, nq), BF16),
        grid=(bsz, nb),
        in_specs=[pl.BlockSpec(memory_space=pltpu.SMEM), pl.BlockSpec((1, blk, nq), lambda b, n: (b, n, 0)),
                  prev, own, nxt, prev, own, nxt, ctx, ctx],
        out_specs=pl.BlockSpec((1, blk, nq), lambda b, n: (b, n, 0)),
        compiler_params=_cparams("parallel", "parallel"),
        name="attn_lat",
    )(sink, q, k, k, k, v, v, v, ck, cv)


def _ssd_proj_kernel(x_ref, xp_ref, xn_ref, a_ref, sh_ref, wz_ref, wx_ref, wdt_ref, cw_ref, cb_ref,
                     z_ref, xs_ref, bm_ref, cm_ref, dt_ref, xe_ref, *, tm, nl, cn, inner, gs):
    xe_ref[...] = _ext_rows(x_ref, xp_ref, xn_ref, a_ref[0], sh_ref[0], nl).astype(BF16)
    nchunk = (inner + 2 * gs) // cn

    def in_proj(j):
        return _dot(xe_ref[...], wx_ref[:, j * cn:(j + 1) * cn])

    nxt = in_proj(0)
    xt = xe_ref[HALO:HALO + tm, :]
    z_ref[0] = _dot(xt, wz_ref[...]).astype(z_ref.dtype)
    dt_ref[0] = _dot(xt, wdt_ref[...])
    for j in range(nchunk):
        sl = slice(j * cn, (j + 1) * cn)
        h = nxt
        if j + 1 < nchunk:
            nxt = in_proj(j + 1)
        y = _silu(_dwconv_rows(h, cw_ref[:, sl], cb_ref[:, sl], 5))[HALO:HALO + tm]
        if j * cn < inner:
            xs_ref[0, :, sl] = y.astype(xs_ref.dtype)
        elif j * cn < inner + gs:
            bm_ref[0, :, j * cn - inner:(j + 1) * cn - inner] = y
        else:
            cm_ref[0, :, j * cn - inner - gs:(j + 1) * cn - inner - gs] = y


def _ssd_project(x, a, sh, w_in, conv_w, conv_b, inner, gs, ndt, tm):
    bsz, seq, d = x.shape
    nl = seq // tm
    cn = 512
    wz = w_in[:, :inner].astype(BF16)
    wx = w_in[:, inner:2 * inner + 2 * gs].astype(BF16)
    wdt = w_in[:, 2 * inner + 2 * gs:].astype(BF16)
    mod_spec = pl.BlockSpec((1, 1, d), lambda b, l: (b, 0, 0))
    sds = lambda n, dt: jax.ShapeDtypeStruct((bsz, seq, n), dt)
    return pl.pallas_call(
        functools.partial(_ssd_proj_kernel, tm=tm, nl=nl, cn=cn, inner=inner, gs=gs),
        out_shape=(sds(inner, BF16), sds(inner, BF16), sds(gs, F32), sds(gs, F32), sds(ndt, F32)),
        grid=(bsz, nl),
        in_specs=_halo_specs(tm, d, nl) + [mod_spec, mod_spec, _const_spec(wz.shape), _const_spec(wx.shape),
                                           _const_spec(wdt.shape), _const_spec(conv_w.shape),
                                           _const_spec((1, conv_b.shape[0]))],
        out_specs=(_row_spec(tm, inner), _row_spec(tm, inner), _row_spec(tm, gs), _row_spec(tm, gs),
                   _row_spec(tm, ndt)),
        scratch_shapes=[pltpu.VMEM((tm + 2 * HALO, d), BF16)],
        compiler_params=_cparams("parallel", "parallel"),
        name="ssd_proj",
    )(x, x, x, a, sh, wz, wx, wdt, conv_w, conv_b[None])


def _tri_masks(reverse):
    i = lax.broadcasted_iota(jnp.int32, (CHUNK, CHUNK), 0)
    j = lax.broadcasted_iota(jnp.int32, (CHUNK, CHUNK), 1)
    tri = (j >= i) if reverse else (j <= i)
    tri_t = (j <= i) if reverse else (j >= i)
    return tri, jnp.where(tri, 1.0, 0.0).astype(F32), jnp.where(tri_t, 1.0, 0.0).astype(F32)


def _cumsum_cols(t_bf, a):
    a1 = a.astype(BF16)
    r1 = a - a1.astype(F32)
    a2 = r1.astype(BF16)
    a3 = (r1 - a2.astype(F32)).astype(BF16)
    return _dot(t_bf, a1) + _dot(t_bf, a2) + _dot(t_bf, a3)


def _ssd_chunk(x_ref, b_ref, c_ref, dtc_raw, dtr_raw, bias_c, bias_r, a_c, a_r, state_ref, y_ref, reverse, hp, ns):
    heads = dtc_raw.shape[1]
    groups = b_ref.shape[2] // ns
    nr = heads // groups
    tri, t_mat, t_mat_t = _tri_masks(reverse)
    dt_c = _softplus(dtc_raw + bias_c)
    dt_r = _softplus(dtr_raw + bias_r)
    acs_c = _cumsum_cols(t_mat.astype(BF16), dt_c * a_c) * LOG2E
    acs_r = _dot_exact(dt_r * a_r, t_mat_t) * LOG2E
    end = 0 if reverse else CHUNK - 1
    last_r = acs_r[:, end:end + 1]
    w_r = dt_r * jnp.exp2(last_r - acs_r)
    e_last = jnp.exp2(last_r)
    hpl = LANES // hp
    lane_head = lax.broadcasted_iota(jnp.int32, (1, LANES), 1) // hp
    for g in range(groups):
        bmat = b_ref[0, :, g * ns:(g + 1) * ns]
        cmat = c_ref[0, :, g * ns:(g + 1) * ns]
        cb = _dot_nt(cmat.astype(BF16), bmat.astype(BF16))
        bt = bmat.T
        for t in range(g * nr // hpl, (g + 1) * nr // hpl):
            cols = slice(t * LANES, (t + 1) * LANES)
            x_t = x_ref[0, :, cols].astype(F32)
            st_t = state_ref[t]
            y_t, upd, e_t = None, None, jnp.zeros((1, LANES), F32)
            for e in range(hpl):
                h = t * hpl + e
                own = lane_head == e
                bc = jnp.broadcast_to(acs_c[:, h:h + 1], (CHUNK, CHUNK))
                bc_n = bc if ns == CHUNK else jnp.broadcast_to(acs_c[:, h:h + 1], (CHUNK, ns))
                mp = cb * jnp.exp2(jnp.where(tri, bc - acs_r[h:h + 1, :], NEG_INF)) * dt_r[h:h + 1, :]
                cp = cmat * jnp.exp2(bc_n)
                xh = jnp.where(own, x_t, 0.0).astype(BF16)
                lhs = jnp.concatenate([mp, cp], axis=1).astype(BF16)
                rhs = jnp.concatenate([xh, jnp.where(own, st_t, 0.0).astype(BF16)], axis=0)
                yh = _dot(lhs, rhs)
                uh = _dot((bt * w_r[h:h + 1, :]).astype(BF16), xh)
                y_t = yh if y_t is None else y_t + yh
                upd = uh if upd is None else upd + uh
                e_t = jnp.where(own, e_last[h:h + 1, :], e_t)
            y_ref[0, :, cols] = y_t.astype(y_ref.dtype)
            state_ref[t] = st_t * e_t + upd


def _ssd_scan_kernel(xf_ref, bf_ref, cf_ref, dcf_ref, drf_ref, xb_ref, bb_ref, cb_ref, dcb_ref, drb_ref,
                     biasc_ref, biasr_ref, ac_ref, ar_ref, h0_ref, yf_ref, yb_ref, hout_ref, state_ref, *, hp, ns):
    c = pl.program_id(1)

    @pl.when(c == 0)
    def _():
        state_ref[...] = h0_ref[0]

    _ssd_chunk(xf_ref, bf_ref, cf_ref, dcf_ref[0, 0], drf_ref[0, 0], biasc_ref[0], biasr_ref[0], ac_ref[0], ar_ref[0],
               state_ref.at[0], yf_ref, False, hp, ns)
    _ssd_chunk(xb_ref, bb_ref, cb_ref, dcb_ref[0, 0], drb_ref[0, 0], biasc_ref[1], biasr_ref[1], ac_ref[1], ar_ref[1],
               state_ref.at[1], yb_ref, True, hp, ns)

    @pl.when(c == pl.num_programs(1) - 1)
    def _():
        hout_ref[0] = state_ref[...]


def _ssd_scan(xs, bm, cm, dt_raw, dt_bias, a_log, h0, groups, hp):
    bsz, seq, inner = xs.shape
    heads = inner // hp
    nr = heads // groups
    ns = bm.shape[-1] // groups
    nc = seq // CHUNK
    gn = groups * ns
    dt4 = dt_raw.reshape(bsz, seq, 2, heads)
    dt_col = dt4.transpose(0, 2, 1, 3)
    dt_row = dt4.transpose(0, 2, 3, 1)
    bias = dt_bias.astype(F32).reshape(2, heads)
    a_neg = (-jnp.exp(a_log.astype(F32))).reshape(2, heads)
    hpl = LANES // hp
    nt = heads // hpl
    h0t = h0.reshape(bsz, 2, nt, hpl, hp, ns).transpose(0, 1, 2, 5, 3, 4).reshape(bsz, 2, nt, ns, LANES)

    def fwd(w):
        return pl.BlockSpec((1, CHUNK, w), lambda b, c: (b, c, 0))

    def bwd(w):
        return pl.BlockSpec((1, CHUNK, w), lambda b, c: (b, nc - 1 - c, 0))

    vec_c = _const_spec((2, 1, heads))
    vec_r = _const_spec((2, heads, 1))
    state_spec = pl.BlockSpec((1, 2, nt, ns, LANES), lambda b, c: (b, 0, 0, 0, 0))
    yf, yb, hout = pl.pallas_call(
        functools.partial(_ssd_scan_kernel, hp=hp, ns=ns),
        out_shape=(jax.ShapeDtypeStruct(xs.shape, BF16), jax.ShapeDtypeStruct(xs.shape, BF16),
                   jax.ShapeDtypeStruct(h0t.shape, F32)),
        grid=(bsz, nc),
        in_specs=[fwd(inner), fwd(gn), fwd(gn),
                  pl.BlockSpec((1, 1, CHUNK, heads), lambda b, c: (b, 0, c, 0)),
                  pl.BlockSpec((1, 1, heads, CHUNK), lambda b, c: (b, 0, 0, c)),
                  bwd(inner), bwd(gn), bwd(gn),
                  pl.BlockSpec((1, 1, CHUNK, heads), lambda b, c: (b, 1, nc - 1 - c, 0)),
                  pl.BlockSpec((1, 1, heads, CHUNK), lambda b, c: (b, 1, 0, nc - 1 - c)),
                  vec_c, vec_r, vec_c, vec_r, state_spec],
        out_specs=(fwd(inner), bwd(inner), state_spec),
        scratch_shapes=[pltpu.VMEM((2, nt, ns, LANES), F32)],
        compiler_params=_cparams("parallel", "arbitrary"),
        name="ssd_scan",
    )(xs, bm, cm, dt_col, dt_row, xs, bm, cm, dt_col, dt_row,
      bias[:, None, :], bias[:, :, None], a_neg[:, None, :], a_neg[:, :, None], h0t)
    hout = hout.reshape(bsz, 2, nt, ns, hpl, hp).transpose(0, 1, 2, 4, 5, 3).reshape(bsz, 2, heads, hp, ns)
    return yf, yb, hout


def _ssd_out_prologue(rows, yf_ref, yb_ref, xs_ref, z_ref, d_ref, g_ref):
    y = yf_ref[0, rows].astype(F32) + yb_ref[0, rows].astype(F32) + xs_ref[0, rows].astype(F32) * d_ref[...]
    y = y * _silu(z_ref[0, rows].astype(F32))
    ms = jnp.mean(y * y, axis=-1, keepdims=True)
    return y * lax.rsqrt(ms + EPS) * g_ref[...]


def _diff_attn_kernel(*refs, hd, tk, has_ctx):
    if has_ctx:
        q_ref, k_ref, v_ref, ck_ref, cv_ref, lam_ref, gsub_ref, o_ref, vt_ref = refs
    else:
        q_ref, k_ref, v_ref, lam_ref, gsub_ref, o_ref, vt_ref = refs
    lk = k_ref.shape[1]
    blocks = [(k_ref, j * tk, j * tk, tk) for j in range(lk // tk)]
    if has_ctx:
        blocks.append((ck_ref, 0, lk, ck_ref.shape[1]))

    @pl.when(pl.program_id(2) == 0)
    def _():
        for j in range(lk // tk):
            vt_ref[:, j * tk:(j + 1) * tk] = v_ref[0, j * tk:(j + 1) * tk, :].astype(F32).T.astype(BF16)
        if has_ctx:
            vt_ref[:, lk:] = cv_ref[0].astype(F32).T.astype(BF16)

    q = q_ref[0].astype(F32)
    lane = lax.broadcasted_iota(jnp.int32, q.shape, 1)
    qz = [jnp.where(lane < hd, q, 0.0).astype(BF16), jnp.where(lane >= hd, q, 0.0).astype(BF16)]
    m, l, acc = [None, None], [None, None], [None, None]
    items = [(j, c) for j in range(len(blocks)) for c in range(2)]

    def scores(j, c):
        ref, r0, _, n = blocks[j]
        return _dot_nt(ref[0, r0:r0 + n, :], qz[c])

    nxt = scores(*items[0])
    for idx, (j, c) in enumerate(items):
        st = nxt
        if idx + 1 < len(items):
            nxt = scores(*items[idx + 1])
        _, _, c0, n = blocks[j]
        vtb = vt_ref[:, c0:c0 + n]
        mx = jnp.max(st, axis=0, keepdims=True)
        if j == 0:
            m[c] = mx
            p = jnp.exp2(st - mx)
            l[c] = jnp.sum(p, axis=0, keepdims=True)
            acc[c] = _dot(vtb, p.astype(BF16))
        else:
            m_new = jnp.maximum(m[c], mx)
            alpha = jnp.exp2(m[c] - m_new)
            p = jnp.exp2(st - m_new)
            l[c] = alpha * l[c] + jnp.sum(p, axis=0, keepdims=True)
            acc[c] = alpha * acc[c] + _dot(vtb, p.astype(BF16))
            m[c] = m_new
    lv = lam_ref[...]
    f1 = jnp.exp(jnp.sum(lv[0:1] * lv[1:2], axis=1, keepdims=True))
    f2 = jnp.exp(jnp.sum(lv[2:3] * lv[3:4], axis=1, keepdims=True))
    lam = f1 - f2 + DIFF_LAMBDA_INIT
    o = (acc[0] / l[0] - lam * (acc[1] / l[1])).T
    ms = jnp.mean(o * o, axis=-1, keepdims=True)
    o_ref[0] = (o * lax.rsqrt(ms + EPS) * gsub_ref[...] * (1.0 - DIFF_LAMBDA_INIT)).astype(o_ref.dtype)


def _diff_attn(q, k, v, ctx, lam_vecs, g_sub, heads, hd, tq, tk):
    bsz, lq, n = q.shape
    lk = k.shape[1]
    vd = 2 * hd
    kv_spec = pl.BlockSpec((1, lk, vd), lambda b, h, i: (b, 0, h))
    ins, specs, ltot = [q, k, v], [pl.BlockSpec((1, tq, vd), lambda b, h, i: (b, i, h)), kv_spec, kv_spec], lk
    if ctx is not None:
        lc = ctx[0].shape[1]
        ins += list(ctx)
        specs += [pl.BlockSpec((1, lc, vd), lambda b, h, i: (b, 0, h))] * 2
        ltot += lc
    return pl.pallas_call(
        functools.partial(_diff_attn_kernel, hd=hd, tk=tk, has_ctx=ctx is not None),
        out_shape=jax.ShapeDtypeStruct((bsz, lq, n), BF16),
        grid=(bsz, heads, lq // tq),
        in_specs=specs + [pl.BlockSpec(lam_vecs.shape, lambda b, h, i: (0, 0)),
                          pl.BlockSpec((1, vd), lambda b, h, i: (0, 0))],
        out_specs=pl.BlockSpec((1, tq, vd), lambda b, h, i: (b, i, h)),
        scratch_shapes=[pltpu.VMEM((vd, ltot), BF16)],
        compiler_params=_cparams("parallel", "parallel", "arbitrary"),
        name="diff_attn",
    )(*ins, lam_vecs, g_sub[None])


def _ml_proj_kernel(x_ref, xp_ref, xn_ref, a_ref, sh_ref, wm_ref, wo_ref, cw_ref, cb_ref, wq_ref, wk_ref, wv_ref,
                    wif_ref, bif_ref, q_ref, k_ref, v_ref, g_ref, op_ref, xe_ref, *, tm, nl, heads, ihd, dk):
    xe_ref[...] = _ext_rows(x_ref, xp_ref, xn_ref, a_ref[0], sh_ref[0], nl).astype(BF16)
    def in_proj(h):
        return _dot(xe_ref[...], wm_ref[:, h * ihd:(h + 1) * ihd])

    nxt = in_proj(0)
    op_ref[0] = _dot(xe_ref[HALO:HALO + tm, :], wo_ref[...]).astype(op_ref.dtype)
    gates = jnp.zeros((tm, bif_ref.shape[1]), F32) + bif_ref[...]
    for h in range(heads):
        sl = slice(h * ihd, (h + 1) * ihd)
        xm_e = nxt
        if h + 1 < heads:
            nxt = in_proj(h + 1)
        xc = _silu(_dwconv_rows(xm_e, cw_ref[:, sl], cb_ref[:, sl], 5))[HALO:HALO + tm].astype(BF16)
        xm = xm_e[HALO:HALO + tm].astype(BF16)
        q_ref[0, :, h * dk:(h + 1) * dk] = _dot(xc, wq_ref[h]).astype(q_ref.dtype)
        k_ref[0, :, h * dk:(h + 1) * dk] = (_dot(xc, wk_ref[h]) * (dk ** -0.5)).astype(k_ref.dtype)
        v_ref[0, :, sl] = _dot(xm, wv_ref[h]).astype(v_ref.dtype)
        gates = gates + _dot(xc, wif_ref[sl, :])
    g_ref[0] = gates


def _ml_project(x, a, sh, w_in, conv_w, conv_b, w_q, w_k, w_v, w_if, b_if, tm):
    bsz, seq, d = x.shape
    heads, ihd, dk = w_q.shape
    inner = heads * ihd
    nl = seq // tm
    ng = w_if.shape[1]
    wm = w_in[:, :inner].astype(BF16)
    wo = w_in[:, inner:].astype(BF16)
    mod_spec = pl.BlockSpec((1, 1, d), lambda b, l: (b, 0, 0))
    sds = lambda n, dt: jax.ShapeDtypeStruct((bsz, seq, n), dt)
    return pl.pallas_call(
        functools.partial(_ml_proj_kernel, tm=tm, nl=nl, heads=heads, ihd=ihd, dk=dk),
        out_shape=(sds(heads * dk, BF16), sds(heads * dk, BF16), sds(inner, BF16), sds(ng, F32), sds(inner, BF16)),
        grid=(bsz, nl),
        in_specs=_halo_specs(tm, d, nl) + [mod_spec, mod_spec, _const_spec(wm.shape), _const_spec(wo.shape),
                                           _const_spec(conv_w.shape), _const_spec((1, inner)),
                                           _const_spec(w_q.shape), _const_spec(w_k.shape), _const_spec(w_v.shape),
                                           _const_spec(w_if.shape), _const_spec((1, ng))],
        out_specs=(_row_spec(tm, heads * dk), _row_spec(tm, heads * dk), _row_spec(tm, inner), _row_spec(tm, ng),
                   _row_spec(tm, inner)),
        scratch_shapes=[pltpu.VMEM((tm + 2 * HALO, d), BF16)],
        compiler_params=_cparams("parallel", "parallel"),
        name="ml_proj",
    )(x, x, x, a, sh, wm, wo, conv_w, conv_b[None], w_q.astype(BF16), w_k.astype(BF16), w_v.astype(BF16),
      w_if.astype(BF16), b_if.reshape(1, ng))


def _ml_gate_free(q_ref, k_ref, c_ref):
    heads, dk, _ = c_ref.shape
    qk, qc = [], []
    for hh in range(heads):
        q = q_ref[0, :, hh * dk:(hh + 1) * dk]
        qk.append(_dot_nt(q, k_ref[0, :, hh * dk:(hh + 1) * dk]))
        qc.append(_dot(q, c_ref[hh].astype(BF16)))
    return qk, qc


def _ml_gate_cumsums(gc, gr, reverse):
    _, t_mat, t_mat_t = _tri_masks(reverse)
    return _cumsum_cols(t_mat.astype(BF16), _log_sigmoid(gc)), _dot_exact(_log_sigmoid(gr), t_mat_t)


def _ml_chunk(q_ref, k_ref, v_ref, gc, gr, d, c_ref, n_ref, m_ref, h_ref, reverse, bcum_call, bcum_rall, qk, qc):
    heads, dk, dv = c_ref.shape
    tri = _tri_masks(reverse)[0]
    end = 0 if reverse else CHUNK - 1
    for hh in range(heads):
        ci, cf = d * 2 * heads + hh, d * 2 * heads + heads + hh
        q = q_ref[0, :, hh * dk:(hh + 1) * dk]
        k = k_ref[0, :, hh * dk:(hh + 1) * dk]
        v = v_ref[0, :, hh * dv:(hh + 1) * dv]
        bcum_c = bcum_call[:, cf:cf + 1]
        bcum_r = bcum_rall[cf:cf + 1, :]
        i_c = gc[:, ci:ci + 1]
        i_r = gr[ci:ci + 1, :]
        m_old = m_ref[hh]
        cmat = c_ref[hh]
        nrow = n_ref[hh]
        dmat = jnp.where(tri, bcum_c - bcum_r + i_r, NEG_INF)
        inter = bcum_c + m_old
        mt = jnp.maximum(jnp.max(dmat, axis=1, keepdims=True), inter)
        s = qk[hh] * jnp.exp(dmat - mt)
        w_int = jnp.exp(inter - mt)
        num = _dot(s.astype(BF16), v) + w_int * qc[hh]
        den = jnp.sum(s, axis=1, keepdims=True) + w_int * jnp.sum(q.astype(F32) * nrow, axis=1, keepdims=True)
        h_ref[0, :, hh * dv:(hh + 1) * dv] = (num / jnp.maximum(jnp.abs(den), jnp.exp(-mt))).astype(h_ref.dtype)
        btot = bcum_c[end:end + 1]
        g = btot - bcum_c + i_c
        m_new = jnp.maximum(btot + m_old, jnp.max(g, axis=0, keepdims=True))
        decay = jnp.exp(btot + m_old - m_new)
        kw = k.astype(F32) * jnp.exp(g - m_new)
        c_ref[hh] = cmat * decay + _dot(kw.T.astype(BF16), v)
        n_ref[hh] = nrow * decay + jnp.sum(kw, axis=0, keepdims=True)
        m_ref[hh] = m_new


def _ml_scan_kernel(qf_ref, kf_ref, vf_ref, gcf_ref, grf_ref, qb_ref, kb_ref, vb_ref, gcb_ref, grb_ref,
                    c0_ref, n0_ref, m0_ref, hf_ref, hb_ref, cout_ref, nout_ref, mout_ref, c_ref, n_ref, m_ref):
    c = pl.program_id(1)

    @pl.when(c == 0)
    def _():
        c_ref[...] = c0_ref[0]
        n_ref[...] = n0_ref[0]
        m_ref[...] = m0_ref[0]

    pre_f = _ml_gate_free(qf_ref, kf_ref, c_ref.at[0])
    pre_b = _ml_gate_free(qb_ref, kb_ref, c_ref.at[1])
    cum_f = _ml_gate_cumsums(gcf_ref[0], grf_ref[0], False)
    _ml_chunk(qf_ref, kf_ref, vf_ref, gcf_ref[0], grf_ref[0], 0, c_ref.at[0], n_ref.at[0], m_ref.at[0], hf_ref, False,
              *cum_f, *pre_f)
    cum_b = _ml_gate_cumsums(gcb_ref[0], grb_ref[0], True)
    _ml_chunk(qb_ref, kb_ref, vb_ref, gcb_ref[0], grb_ref[0], 1, c_ref.at[1], n_ref.at[1], m_ref.at[1], hb_ref, True,
              *cum_b, *pre_b)

    @pl.when(c == pl.num_programs(1) - 1)
    def _():
        cout_ref[0] = c_ref[...]
        nout_ref[0] = n_ref[...]
        mout_ref[0] = m_ref[...]


def _ml_scan(q, k, v, gates, c0, n0, m0):
    bsz, seq, _ = q.shape
    _, _, heads, dk, dv = c0.shape
    ng = gates.shape[-1]
    nc = seq // CHUNK
    g_row = gates.transpose(0, 2, 1)
    n0 = n0.reshape(bsz, 2, heads, 1, dk)
    m0 = m0.reshape(bsz, 2, heads, 1, 1)

    def fwd(w):
        return pl.BlockSpec((1, CHUNK, w), lambda b, c: (b, c, 0))

    def bwd(w):
        return pl.BlockSpec((1, CHUNK, w), lambda b, c: (b, nc - 1 - c, 0))

    c_spec = pl.BlockSpec((1, 2, heads, dk, dv), lambda b, c: (b, 0, 0, 0, 0))
    n_spec = pl.BlockSpec((1, 2, heads, 1, dk), lambda b, c: (b, 0, 0, 0, 0))
    m_spec = pl.BlockSpec((1, 2, heads, 1, 1), lambda b, c: (b, 0, 0, 0, 0))
    hf, hb, cout, nout, mout = pl.pallas_call(
        _ml_scan_kernel,
        out_shape=(jax.ShapeDtypeStruct(v.shape, BF16), jax.ShapeDtypeStruct(v.shape, BF16),
                   jax.ShapeDtypeStruct(c0.shape, F32), jax.ShapeDtypeStruct(n0.shape, F32),
                   jax.ShapeDtypeStruct(m0.shape, F32)),
        grid=(bsz, nc),
        in_specs=[fwd(heads * dk), fwd(heads * dk), fwd(heads * dv), fwd(ng),
                  pl.BlockSpec((1, ng, CHUNK), lambda b, c: (b, 0, c)),
                  bwd(heads * dk), bwd(heads * dk), bwd(heads * dv), bwd(ng),
                  pl.BlockSpec((1, ng, CHUNK), lambda b, c: (b, 0, nc - 1 - c)),
                  c_spec, n_spec, m_spec],
        out_specs=(fwd(heads * dv), bwd(heads * dv), c_spec, n_spec, m_spec),
        scratch_shapes=[pltpu.VMEM((2, heads, dk, dv), F32), pltpu.VMEM((2, heads, 1, dk), F32),
                        pltpu.VMEM((2, heads, 1, 1), F32)],
        compiler_params=_cparams("parallel", "arbitrary"),
        name="ml_scan",
    )(q, k, v, gates, g_row, q, k, v, gates, g_row, c0, n0, m0)
    return hf, hb, cout, nout.reshape(bsz, 2, heads, dk), mout.reshape(bsz, 2, heads)


def _ml_out_prologue(rows, hf_ref, hb_ref, op_ref, g_ref, *, heads, dv):
    h = hf_ref[0, rows].astype(F32) + hb_ref[0, rows].astype(F32)
    outs = []
    for hh in range(heads):
        blk = h[:, hh * dv:(hh + 1) * dv]
        ms = jnp.mean(blk * blk, axis=-1, keepdims=True)
        outs.append(blk * lax.rsqrt(ms + EPS))
    return jax.nn.sigmoid(op_ref[0, rows].astype(F32)) * (jnp.concatenate(outs, axis=1) * g_ref[...])


def _stream_mods(mod, g1, g2):
    a1 = (g1 * (1.0 + mod[:, 1]))[:, None]
    a2 = (g2 * (1.0 + mod[:, 4]))[:, None]
    return (a1, mod[:, 0][:, None], mod[:, 2][:, None]), (a2, mod[:, 3][:, None], mod[:, 5][:, None])


def _tile(seq):
    return min(seq, 512)


def _attn_layer(x, mods, lat_cache, w_qkv, g_q, g_k, sink, w_o):
    a, sh, gt = mods
    bsz, seq, d = x.shape
    hd = g_q.shape[0]
    heads = sink.shape[0]
    kvh = (w_qkv.shape[1] // hd - heads) // 2
    tm = _tile(seq)
    gq = jnp.tile(g_q, LANES // hd)[None]
    gk = jnp.tile(g_k, LANES // hd)[None]
    qscale = hd ** -0.5 * LOG2E
    if lat_cache is None:
        q, k, v = _qkv_project(x, a, sh, w_qkv, gq, gk, heads * hd, kvh * hd, kvh * hd, None, F32, tm, qscale)
        o = _attn_ctx(q, k, v, sink, kvh, heads // kvh, hd)
    else:
        ck, cv = lat_cache
        q, k, v = _qkv_project(x, a, sh, w_qkv, gq, gk, heads * hd, kvh * hd, kvh * hd, _rope_tables(seq, hd), BF16, tm,
                               qscale)
        o = _attn_lat(q, k, v, ck.reshape(bsz, -1, kvh * hd), cv.reshape(bsz, -1, kvh * hd), sink, kvh,
                      heads // kvh, hd, 128)
    x = _oproj(_rows_prologue, [o],[_row_spec(tm, heads * hd)], x, gt, w_o, tm)
    return x, k.reshape(bsz, seq, kvh, hd), v.reshape(bsz, seq, kvh, hd)


def _ssd_layer(x, mods, h0, w_in, conv_w, conv_b, dt_bias, a_log, d_skip, g_norm, w_out):
    a, sh, gt = mods
    bsz, seq, d = x.shape
    heads = d_skip.shape[0]
    inner = g_norm.shape[0]
    hp = inner // heads
    gs = (conv_w.shape[1] - inner) // 2
    groups = gs // LANES
    tm = _tile(seq)
    z, xs, bm, cm, dt_raw = _ssd_project(x, a, sh, w_in, conv_w, conv_b, inner, gs, 2 * heads, tm)
    yf, yb, hout = _ssd_scan(xs, bm, cm, dt_raw, dt_bias, a_log, h0, groups, hp)
    rs = _row_spec(tm, inner)
    x = _oproj(_ssd_out_prologue, [yf, yb, xs, z, jnp.repeat(d_skip, hp)[None], g_norm[None]],
               [rs, rs, rs, rs, _const_spec((1, inner)), _const_spec((1, inner))], x, gt, w_out, tm)
    return x, hout


def _diff_layer(x, mods, lat_cache, w_qkv, g_q, g_k, lam_vecs, g_sub, w_o):
    a, sh, gt = mods
    bsz, seq, d = x.shape
    hd = g_q.shape[1]
    vd = g_sub.shape[0]
    heads = w_qkv.shape[1] // (4 * hd + vd)
    nq = heads * 2 * hd
    tm = _tile(seq)
    gq = g_q.reshape(1, 2 * hd)
    gk = g_k.reshape(1, 2 * hd)
    qscale = hd ** -0.5 * math.log2(math.e)
    if lat_cache is None:
        q, k, v = _qkv_project(x, a, sh, w_qkv, gq, gk, nq, nq, heads * vd, None, F32, tm, qscale)
        o = _diff_attn(q, k.astype(BF16), v.astype(BF16), None, lam_vecs, g_sub, heads, hd, seq, seq)
    else:
        ck, cv = lat_cache
        q, k, v = _qkv_project(x, a, sh, w_qkv, gq, gk, nq, nq, heads * vd, _rope_tables(seq, hd), BF16, tm, qscale)
        ctx = (ck.reshape(bsz, -1, nq).astype(BF16), cv.reshape(bsz, -1, heads * vd).astype(BF16))
        o = _diff_attn(q, k, v, ctx, lam_vecs, g_sub, heads, hd, tm, min(seq, 1024))
    x = _oproj(_rows_prologue, [o],[_row_spec(tm, heads * vd)], x, gt, w_o, tm)
    return x, k.reshape(bsz, seq, heads, 2, hd), v.reshape(bsz, seq, heads, vd)


def _ml_layer(x, mods, c0, n0, m0, w_in, conv_w, conv_b, w_q, w_k, w_v, w_if, b_if, g_norm, w_out):
    a, sh, gt = mods
    bsz, seq, d = x.shape
    heads, dv = g_norm.shape
    inner = heads * dv
    tm = _tile(seq)
    q, k, v, gates, o_pre = _ml_project(x, a, sh, w_in, conv_w, conv_b, w_q, w_k, w_v, w_if, b_if, tm)
    hf, hb, cout, nout, mout = _ml_scan(q, k, v, gates, c0, n0, m0)
    rs = _row_spec(tm, inner)
    x = _oproj(functools.partial(_ml_out_prologue, heads=heads, dv=dv), [hf, hb, o_pre, g_norm.reshape(1, inner)],
               [rs, rs, rs, _const_spec((1, inner))], x, gt, w_out, tm)
    return x, cout, nout, mout


def kernel(x_prompt, x_sample, cache_attn_k, cache_attn_v, state_ssd, cache_diff_k, cache_diff_v, state_mlstm_c, state_mlstm_n, state_mlstm_m, c, c_ctx, ada_w, ada_b, norm1_g, norm2_g, ffn_w_up, ffn_conv_w, ffn_conv_b, ffn_w_down, attn_w_qkv, attn_g_q, attn_g_k, attn_sink, attn_w_o, ssd_w_in, ssd_conv_w, ssd_conv_b, ssd_dt_bias, ssd_a_log, ssd_d, ssd_g_norm, ssd_w_out, diff_w_qkv, diff_g_q, diff_g_k, diff_lq1, diff_lk1, diff_lq2, diff_lk2, diff_g_sub, diff_w_o, ml_w_in, ml_conv_w, ml_conv_b, ml_w_q, ml_w_k, ml_w_v, ml_w_if, ml_b_if, ml_g_norm, ml_w_out):
    xp, xs = x_prompt, x_sample
    bp, bs = xp.shape[0], xs.shape[0]
    d = xp.shape[-1]
    depth = ada_w.shape[0]
    rows = 16
    cond = jnp.concatenate([c_ctx[None], c, jnp.zeros((rows - 1 - bs, d), F32)], axis=0)
    mod = _ada(cond, ada_w, ada_b).reshape(depth, rows, 6, d)
    lam_vecs = jnp.stack([diff_lq1, diff_lk1, diff_lq2, diff_lk2])
    outs = {}
    for i in range(depth):
        mp1, mp2 = _stream_mods(jnp.broadcast_to(mod[i, 0], (bp, 6, d)), norm1_g[i], norm2_g[i])
        ms1, ms2 = _stream_mods(mod[i, 1:1 + bs], norm1_g[i], norm2_g[i])
        kind = i % 4
        if kind == 0:
            xp, outs["ak"], outs["av"] = _attn_layer(xp, mp1, None, attn_w_qkv, attn_g_q, attn_g_k, attn_sink, attn_w_o)
            xs, _, _ = _attn_layer(xs, ms1, (cache_attn_k, cache_attn_v), attn_w_qkv, attn_g_q, attn_g_k, attn_sink,
                                   attn_w_o)
        elif kind == 1:
            ssd_w = (ssd_w_in, ssd_conv_w, ssd_conv_b, ssd_dt_bias, ssd_a_log, ssd_d, ssd_g_norm, ssd_w_out)
            xp, outs["ssd"] = _ssd_layer(xp, mp1, jnp.zeros((bp,) + state_ssd.shape[1:], F32), *ssd_w)
            xs, _ = _ssd_layer(xs, ms1, state_ssd, *ssd_w)
        elif kind == 2:
            diff_w = (diff_w_qkv, diff_g_q, diff_g_k, lam_vecs, diff_g_sub, diff_w_o)
            xp, outs["dk"], outs["dv"] = _diff_layer(xp, mp1, None, *diff_w)
            xs, _, _ = _diff_layer(xs, ms1, (cache_diff_k, cache_diff_v), *diff_w)
        else:
            ml_w = (ml_w_in, ml_conv_w, ml_conv_b, ml_w_q, ml_w_k, ml_w_v, ml_w_if, ml_b_if, ml_g_norm, ml_w_out)
            zc = jnp.zeros((bp,) + state_mlstm_c.shape[1:], F32)
            zn = jnp.zeros((bp,) + state_mlstm_n.shape[1:], F32)
            zm = jnp.zeros((bp,) + state_mlstm_m.shape[1:], F32)
            xp, outs["mc"], outs["mn"], outs["mm"] = _ml_layer(xp, mp1, zc, zn, zm, *ml_w)
            xs, _, _, _ = _ml_layer(xs, ms1, state_mlstm_c, state_mlstm_n, state_mlstm_m, *ml_w)
        ffn_w = (ffn_w_up[i], ffn_conv_w[i], ffn_conv_b[i], ffn_w_down[i])
        xp = _ffn(xp, *mp2, *ffn_w, tm=_tile(xp.shape[1]))
        xs = _ffn(xs, *ms2, *ffn_w, tm=_tile(xs.shape[1]))
    return (xp, xs, outs["ak"], outs["av"], outs["ssd"], outs["dk"], outs["dv"], outs["mc"], outs["mn"], outs["mm"])
```

```python
import functools
import math

import jax
import jax.numpy as jnp
from jax import lax
from jax.experimental import pallas as pl
from jax.experimental.pallas import tpu as pltpu

F32 = jnp.float32
BF16 = jnp.bfloat16
HIGHEST = lax.Precision.HIGHEST

EPS = 1e-6
ROPE_THETA = 10000.0
GRID_W = 64
DIFF_LAMBDA_INIT = 0.8 - 0.6 * math.exp(-0.3 * 2)

VMEM_LIMIT_BYTES = 56 * 1024 * 1024
LANES = 128
MXU_ROWS = 256
BF16_SUBLANES = 16
HALO = BF16_SUBLANES
CHUNK = 128
SCORE_LOOKAHEAD = 2
FFN_LOOKAHEAD = 3
NEG_INF = float("-inf")
LOG2E = math.log2(math.e)


def _cparams(*sem):
    return pltpu.CompilerParams(dimension_semantics=sem, vmem_limit_bytes=VMEM_LIMIT_BYTES)


def _dot(a, b):
    return jnp.dot(a, b, preferred_element_type=F32)


def _dot_nt(a, b):
    return lax.dot_general(a, b, (((1,), (1,)), ((), ())), preferred_element_type=F32)


def _dot_exact(a, b):
    return jnp.dot(a, b, preferred_element_type=F32, precision=HIGHEST)


def _norm_mod(x, a, sh):
    ms = jnp.mean(x * x, axis=-1, keepdims=True)
    return x * lax.rsqrt(ms + EPS) * a + sh


def _silu(x):
    return x * jax.nn.sigmoid(x)


def _softplus(x):
    return jnp.maximum(x, 0.0) + jnp.log(1.0 + jnp.exp(-jnp.abs(x)))


def _log_sigmoid(x):
    return jnp.minimum(x, 0.0) - jnp.log(1.0 + jnp.exp(-jnp.abs(x)))


def _const_spec(shape):
    nd = len(shape)
    return pl.BlockSpec(shape, lambda *_: (0,) * nd)


def _ext_rows(x_ref, xp_ref, xn_ref, a, sh, nl):
    l = pl.program_id(1)
    xt = _norm_mod(x_ref[0], a, sh)
    xp = _norm_mod(xp_ref[0], a, sh) * (l > 0).astype(F32)
    xn = _norm_mod(xn_ref[0], a, sh) * (l < nl - 1).astype(F32)
    return jnp.concatenate([xp, xt, xn], axis=0)


def _halo_specs(tm, d, nl):
    r = tm // HALO
    return [
        pl.BlockSpec((1, tm, d), lambda b, l: (b, l, 0)),
        pl.BlockSpec((1, HALO, d), lambda b, l: (b, jnp.maximum(l * r - 1, 0), 0)),
        pl.BlockSpec((1, HALO, d), lambda b, l: (b, jnp.minimum((l + 1) * r, nl * r - 1), 0)),
    ]


def _row_blocks(tm):
    n = max(tm // MXU_ROWS, 1)
    return [slice(i * (tm // n), (i + 1) * (tm // n)) for i in range(n)]


def _dwconv_rows(h, w, b, width):
    rows = h.shape[0]
    pad = width // 2
    y = h * w[pad:pad + 1]
    for k in range(width):
        if k == pad:
            continue
        y = y + pltpu.roll(h, (pad - k) % rows, 0) * w[k:k + 1]
    return y + b


def _ada_kernel(c_ref, w_ref, b_ref, o_ref):
    s = _silu(c_ref[...]).astype(BF16)
    o_ref[0] = _dot(s, w_ref[0].astype(BF16)) + b_ref[0]


def _ada(cond, ada_w, ada_b):
    depth, d, n = ada_w.shape
    r = cond.shape[0]
    tn = 1536
    return pl.pallas_call(
        _ada_kernel,
        out_shape=jax.ShapeDtypeStruct((depth, r, n), F32),
        grid=(depth, n // tn),
        in_specs=[
            pl.BlockSpec((r, d), lambda i, j: (0, 0)),
            pl.BlockSpec((1, d, tn), lambda i, j: (i, 0, j)),
            pl.BlockSpec((1, 1, tn), lambda i, j: (i, 0, j)),
        ],
        out_specs=pl.BlockSpec((1, r, tn), lambda i, j: (i, 0, j)),
        compiler_params=_cparams("parallel", "parallel"),
        name="ada",
    )(cond, ada_w, ada_b.reshape(depth, 1, n))


def _ffn_kernel(x_ref, xp_ref, xn_ref, a_ref, sh_ref, gt_ref, wu_ref, cw_ref, cb_ref, wd_ref, o_ref, xe_ref, acc_ref,
                *, tm, nl, nchunk):
    xe_ref[...] = _ext_rows(x_ref, xp_ref, xn_ref, a_ref[0], sh_ref[0], nl).astype(BF16)

    dff = wd_ref.shape[0]
    cn = dff // nchunk

    def cols(j, half):
        return slice(half * dff + j * cn, half * dff + (j + 1) * cn)

    def up(j):
        xe = xe_ref[...]
        return _dot(xe, wu_ref[:, cols(j, 0)]), _dot(xe, wu_ref[:, cols(j, 1)])

    ahead = [up(j) for j in range(min(FFN_LOOKAHEAD, nchunk))]
    for j in range(nchunk):
        ua, ug = ahead.pop(0)
        if j + FFN_LOOKAHEAD < nchunk:
            ahead.append(up(j + FFN_LOOKAHEAD))
        ha = _dwconv_rows(ua, cw_ref[:, cols(j, 0)], cb_ref[:, cols(j, 0)], 3)[HALO:HALO + tm]
        hg = _dwconv_rows(ug, cw_ref[:, cols(j, 1)], cb_ref[:, cols(j, 1)], 3)[HALO:HALO + tm]
        part = _dot((ha * _silu(hg)).astype(BF16), wd_ref[j * cn:(j + 1) * cn, :])
        if j == 0:
            acc_ref[...] = part
        else:
            acc_ref[...] += part
    o_ref[0] = x_ref[0] + gt_ref[0] * acc_ref[...]


def _ffn(x, a, sh, gt, w_up, conv_w, conv_b, w_down, tm):
    bsz, seq, d = x.shape
    dff = w_down.shape[0]
    nchunk = dff // MXU_ROWS
    nl = seq // tm
    mod_spec = pl.BlockSpec((1, 1, d), lambda b, l: (b, 0, 0))
    return pl.pallas_call(
        functools.partial(_ffn_kernel, tm=tm, nl=nl, nchunk=nchunk),
        out_shape=jax.ShapeDtypeStruct(x.shape, F32),
        grid=(bsz, nl),
        in_specs=_halo_specs(tm, d, nl) + [mod_spec, mod_spec, mod_spec, _const_spec(w_up.shape),
                                           _const_spec(conv_w.shape), _const_spec((1, 2 * dff)),
                                           _const_spec(w_down.shape)],
        out_specs=pl.BlockSpec((1, tm, d), lambda b, l: (b, l, 0)),
        scratch_shapes=[pltpu.VMEM((tm + 2 * HALO, d), BF16), pltpu.VMEM((tm, d), F32)],
        compiler_params=_cparams("parallel", "parallel"),
        name="ffn",
    )(x, x, x, a, sh, gt, w_up.astype(BF16), conv_w, conv_b[None], w_down.astype(BF16))


def _oproj_kernel(*refs, prologue, n_in):
    in_refs = refs[:n_in]
    x_ref, gt_ref, w_ref, o_ref = refs[n_in:]
    for rows in _row_blocks(x_ref.shape[1]):
        lhs = prologue(rows, *in_refs)
        o_ref[0, rows] = x_ref[0, rows] + gt_ref[0] * _dot(lhs.astype(BF16), w_ref[...])


def _oproj(prologue, ins, in_specs, x, gt, w, tm):
    bsz, seq, d = x.shape
    return pl.pallas_call(
        functools.partial(_oproj_kernel, prologue=prologue, n_in=len(ins)),
        out_shape=jax.ShapeDtypeStruct(x.shape, F32),
        grid=(bsz, seq // tm),
        in_specs=list(in_specs) + [pl.BlockSpec((1, tm, d), lambda b, l: (b, l, 0)),
                                   pl.BlockSpec((1, 1, d), lambda b, l: (b, 0, 0)),
                                   _const_spec(w.shape)],
        out_specs=pl.BlockSpec((1, tm, d), lambda b, l: (b, l, 0)),
        compiler_params=_cparams("parallel", "parallel"),
        name="oproj",
    )(*ins, x, gt, w.astype(BF16))


def _rows_prologue(rows, o_ref):
    return o_ref[0, rows]


def _row_spec(tm, n):
    return pl.BlockSpec((1, tm, n), lambda b, l: (b, l, 0))


def _head_sumsq_matrix(hd):
    i = lax.broadcasted_iota(jnp.int32, (LANES, LANES), 0) // hd
    j = lax.broadcasted_iota(jnp.int32, (LANES, LANES), 1) // hd
    return jnp.where(i == j, 1.0 / hd, 0.0).astype(BF16)


def _qk_norm_rope(y, gain, rope, avg):
    outs = []
    for c in range(y.shape[1] // LANES):
        yb = y[:, c * LANES:(c + 1) * LANES]
        ms = _dot((yb * yb).astype(BF16), avg)
        yb = yb * lax.rsqrt(ms + EPS) * gain
        if rope is not None:
            cos, s_lo, s_hi = rope
            yb = yb * cos + pltpu.roll(yb, LANES - 32, 1) * s_lo + pltpu.roll(yb, 32, 1) * s_hi
        outs.append(yb)
    return outs


def _rope_tables(seq, hd):
    rows = seq // GRID_W
    row = jnp.repeat(jnp.arange(rows, dtype=F32), GRID_W)
    col = jnp.tile(jnp.arange(GRID_W, dtype=F32), rows)
    nf = hd // 4
    inv = ROPE_THETA ** (-jnp.arange(nf, dtype=F32) / nf)
    ang = jnp.concatenate([row[:, None] * inv, col[:, None] * inv], axis=-1)
    cos, sin = jnp.cos(ang), jnp.sin(ang)
    zero = jnp.zeros_like(sin)
    rep = LANES // hd
    cos_t = jnp.tile(jnp.concatenate([cos, cos], axis=-1), (1, rep))
    s_lo = jnp.tile(jnp.concatenate([-sin, zero], axis=-1), (1, rep))
    s_hi = jnp.tile(jnp.concatenate([zero, sin], axis=-1), (1, rep))
    return cos_t, s_lo, s_hi


def _attn_qkv_kernel(*refs, use_rope, nq, nk, qscale):
    if use_rope:
        x_ref, a_ref, sh_ref, w_ref, gq_ref, gk_ref, cos_ref, slo_ref, shi_ref, q_ref, k_ref, v_ref = refs
        rope = (cos_ref, slo_ref, shi_ref)
    else:
        x_ref, a_ref, sh_ref, w_ref, gq_ref, gk_ref, q_ref, k_ref, v_ref = refs
        rope = None
    avg = _head_sumsq_matrix(64)
    for rows in _row_blocks(x_ref.shape[1]):
        xn = _norm_mod(x_ref[0, rows], a_ref[0], sh_ref[0]).astype(BF16)
        y = _dot(xn, w_ref[...])
        rope_rows = None if rope is None else tuple(t[rows] for t in rope)
        q = _qk_norm_rope(y[:, :nq], gq_ref[...], rope_rows, avg)
        k = _qk_norm_rope(y[:, nq:nq + nk], gk_ref[...], rope_rows, avg)
        for c, blk in enumerate(q):
            q_ref[0, rows, c * LANES:(c + 1) * LANES] = (blk * qscale).astype(q_ref.dtype)
        for c, blk in enumerate(k):
            k_ref[0, rows, c * LANES:(c + 1) * LANES] = blk.astype(k_ref.dtype)
        v_ref[0, rows] = y[:, nq + nk:].astype(v_ref.dtype)


def _qkv_project(x, a, sh, w, gq, gk, nq, nk, nv, rope, kv_dtype, tm, qscale):
    bsz, seq, d = x.shape
    mod_spec = pl.BlockSpec((1, 1, d), lambda b, l: (b, 0, 0))
    ins = [x, a, sh, w.astype(BF16), gq, gk]
    specs = [_row_spec(tm, d), mod_spec, mod_spec, _const_spec(w.shape), _const_spec(gq.shape), _const_spec(gk.shape)]
    if rope is not None:
        ins += list(rope)
        specs += [pl.BlockSpec((tm, LANES), lambda b, l: (l, 0))] * 3
    return pl.pallas_call(
        functools.partial(_attn_qkv_kernel, use_rope=rope is not None, nq=nq, nk=nk, qscale=qscale),
        out_shape=(jax.ShapeDtypeStruct((bsz, seq, nq), BF16),
                   jax.ShapeDtypeStruct((bsz, seq, nk), kv_dtype),
                   jax.ShapeDtypeStruct((bsz, seq, nv), kv_dtype)),
        grid=(bsz, seq // tm),
        in_specs=specs,
        out_specs=(_row_spec(tm, nq), _row_spec(tm, nk), _row_spec(tm, nv)),
        compiler_params=_cparams("parallel", "parallel"),
        name="qkv",
    )(*ins)


def _sink_softmax_pv(parts, sink2):
    m = sink2
    for s, _ in parts:
        m = jnp.maximum(m, jnp.max(s, axis=1, keepdims=True))
    den = jnp.exp2(sink2 - m)
    acc = None
    for s, v in parts:
        e = jnp.exp2(s - m)
        den = den + jnp.sum(e, axis=1, keepdims=True)
        pv = _dot(e.astype(BF16), v)
        acc = pv if acc is None else acc + pv
    return acc / den


def _attn_ctx_kernel(sink_ref, q_ref, k_ref, v_ref, o_ref, *, kvh, group, hd):
    for g in range(kvh):
        kh = k_ref[0, :, g * hd:(g + 1) * hd].astype(BF16)
        vh = v_ref[0, :, g * hd:(g + 1) * hd].astype(BF16)
        for r in range(group):
            h = g * group + r
            s = _dot_nt(q_ref[0, :, h * hd:(h + 1) * hd], kh)
            o_ref[0, :, h * hd:(h + 1) * hd] = _sink_softmax_pv([(s, vh)], sink_ref[h] * LOG2E).astype(o_ref.dtype)


def _attn_ctx(q, k, v, sink, kvh, group, hd):
    bsz, seq, nq = q.shape
    nk = k.shape[-1]
    return pl.pallas_call(
        functools.partial(_attn_ctx_kernel, kvh=kvh, group=group, hd=hd),
        out_shape=jax.ShapeDtypeStruct((bsz, seq, nq), BF16),
        grid=(bsz,),
        in_specs=[pl.BlockSpec(memory_space=pltpu.SMEM),
                  pl.BlockSpec((1, seq, nq), lambda b: (b, 0, 0)),
                  pl.BlockSpec((1, seq, nk), lambda b: (b, 0, 0)),
                  pl.BlockSpec((1, seq, nk), lambda b: (b, 0, 0))],
        out_specs=pl.BlockSpec((1, seq, nq), lambda b: (b, 0, 0)),
        compiler_params=_cparams("parallel"),
        name="attn_ctx",
    )(sink, q, k, v)


def _attn_lat_kernel(sink_ref, q_ref, kp_ref, ko_ref, kn_ref, vp_ref, vo_ref, vn_ref, ck_ref, cv_ref, o_ref, *,
                     kvh, group, hd, blk):
    n = pl.program_id(1)
    nb = pl.num_programs(1)
    qi = lax.broadcasted_iota(jnp.int32, (blk, 3 * blk), 0)
    kj = lax.broadcasted_iota(jnp.int32, (blk, 3 * blk), 1)
    ok = (kj >= qi) & (kj <= qi + 2 * blk)
    ok = ok & ((kj >= blk) | (n > 0)) & ((kj < 2 * blk) | (n < nb - 1))
    bias = jnp.where(ok, 0.0, NEG_INF)
    def keys_values(g):
        sl = slice(g * hd, (g + 1) * hd)
        k_loc = jnp.concatenate([kp_ref[0, :, sl], ko_ref[0, :, sl], kn_ref[0, :, sl]], axis=0)
        v_loc = jnp.concatenate([vp_ref[0, :, sl], vo_ref[0, :, sl], vn_ref[0, :, sl]], axis=0)
        return k_loc, ck_ref[0, :, sl].astype(BF16), v_loc, cv_ref[0, :, sl].astype(BF16)

    def scores(h, kv):
        qh = q_ref[0, :, h * hd:(h + 1) * hd]
        return _dot_nt(qh, kv[0]) + bias, _dot_nt(qh, kv[1])

    kvs = {}

    def kv_of(h):
        if h // group not in kvs:
            kvs[h // group] = keys_values(h // group)
        return kvs[h // group]

    nh = kvh * group
    ahead = [scores(h, kv_of(h)) for h in range(min(SCORE_LOOKAHEAD, nh))]
    for h in range(nh):
        s_loc, s_ctx = ahead.pop(0)
        if h + SCORE_LOOKAHEAD < nh:
            ahead.append(scores(h + SCORE_LOOKAHEAD, kv_of(h + SCORE_LOOKAHEAD)))
        v_loc, v_ctx = kv_of(h)[2], kv_of(h)[3]
        o = _sink_softmax_pv([(s_loc, v_loc), (s_ctx, v_ctx)], sink_ref[h] * LOG2E)
        o_ref[0, :, h * hd:(h + 1) * hd] = o.astype(o_ref.dtype)


def _attn_lat(q, k, v, ck, cv, sink, kvh, group, hd, blk):
    bsz, seq, nq = q.shape
    nk = k.shape[-1]
    nctx = ck.shape[1]
    nb = seq // blk
    prev = pl.BlockSpec((1, blk, nk), lambda b, n: (b, jnp.maximum(n - 1, 0), 0))
    own = pl.BlockSpec((1, blk, nk), lambda b, n: (b, n, 0))
    nxt = pl.BlockSpec((1, blk, nk), lambda b, n: (b, jnp.minimum(n + 1, nb - 1), 0))
    ctx = pl.BlockSpec((1, nctx, nk), lambda b, n: (b, 0, 0))
    return pl.pallas_call(
        functools.partial(_attn_lat_kernel, kvh=kvh, group=group, hd=hd, blk=blk),
        out_shape=jax.ShapeDtypeStruct((bsz, seq, nq), BF16),
        grid=(bsz, nb),
        in_specs=[pl.BlockSpec(memory_space=pltpu.SMEM), pl.BlockSpec((1, blk, nq), lambda b, n: (b, n, 0)),
                  prev, own, nxt, prev, own, nxt, ctx, ctx],
        out_specs=pl.BlockSpec((1, blk, nq), lambda b, n: (b, n, 0)),
        compiler_params=_cparams("parallel", "parallel"),
        name="attn_lat",
    )(sink, q, k, k, k, v, v, v, ck, cv)


def _ssd_proj_kernel(x_ref, xp_ref, xn_ref, a_ref, sh_ref, wz_ref, wx_ref, wdt_ref, cw_ref, cb_ref,
                     z_ref, xs_ref, bm_ref, cm_ref, dt_ref, xe_ref, *, tm, nl, cn, inner, gs):
    xe_ref[...] = _ext_rows(x_ref, xp_ref, xn_ref, a_ref[0], sh_ref[0], nl).astype(BF16)
    nchunk = (inner + 2 * gs) // cn

    def in_proj(j):
        return _dot(xe_ref[...], wx_ref[:, j * cn:(j + 1) * cn])

    nxt = in_proj(0)
    xt = xe_ref[HALO:HALO + tm, :]
    z_ref[0] = _dot(xt, wz_ref[...]).astype(z_ref.dtype)
    dt_ref[0] = _dot(xt, wdt_ref[...])
    for j in range(nchunk):
        sl = slice(j * cn, (j + 1) * cn)
        h = nxt
        if j + 1 < nchunk:
            nxt = in_proj(j + 1)
        y = _silu(_dwconv_rows(h, cw_ref[:, sl], cb_ref[:, sl], 5))[HALO:HALO + tm]
        if j * cn < inner:
            xs_ref[0, :, sl] = y.astype(xs_ref.dtype)
        elif j * cn < inner + gs:
            bm_ref[0, :, j * cn - inner:(j + 1) * cn - inner] = y
        else:
            cm_ref[0, :, j * cn - inner - gs:(j + 1) * cn - inner - gs] = y


def _ssd_project(x, a, sh, w_in, conv_w, conv_b, inner, gs, ndt, tm):
    bsz, seq, d = x.shape
    nl = seq // tm
    cn = 512
    wz = w_in[:, :inner].astype(BF16)
    wx = w_in[:, inner:2 * inner + 2 * gs].astype(BF16)
    wdt = w_in[:, 2 * inner + 2 * gs:].astype(BF16)
    mod_spec = pl.BlockSpec((1, 1, d), lambda b, l: (b, 0, 0))
    sds = lambda n, dt: jax.ShapeDtypeStruct((bsz, seq, n), dt)
    return pl.pallas_call(
        functools.partial(_ssd_proj_kernel, tm=tm, nl=nl, cn=cn, inner=inner, gs=gs),
        out_shape=(sds(inner, BF16), sds(inner, BF16), sds(gs, F32), sds(gs, F32), sds(ndt, F32)),
        grid=(bsz, nl),
        in_specs=_halo_specs(tm, d, nl) + [mod_spec, mod_spec, _const_spec(wz.shape), _const_spec(wx.shape),
                                           _const_spec(wdt.shape), _const_spec(conv_w.shape),
                                           _const_spec((1, conv_b.shape[0]))],
        out_specs=(_row_spec(tm, inner), _row_spec(tm, inner), _row_spec(tm, gs), _row_spec(tm, gs),
                   _row_spec(tm, ndt)),
        scratch_shapes=[pltpu.VMEM((tm + 2 * HALO, d), BF16)],
        compiler_params=_cparams("parallel", "parallel"),
        name="ssd_proj",
    )(x, x, x, a, sh, wz, wx, wdt, conv_w, conv_b[None])


def _tri_masks(reverse):
    i = lax.broadcasted_iota(jnp.int32, (CHUNK, CHUNK), 0)
    j = lax.broadcasted_iota(jnp.int32, (CHUNK, CHUNK), 1)
    tri = (j >= i) if reverse else (j <= i)
    tri_t = (j <= i) if reverse else (j >= i)
    return tri, jnp.where(tri, 1.0, 0.0).astype(F32), jnp.where(tri_t, 1.0, 0.0).astype(F32)


def _cumsum_cols(t_bf, a):
    a1 = a.astype(BF16)
    r1 = a - a1.astype(F32)
    a2 = r1.astype(BF16)
    a3 = (r1 - a2.astype(F32)).astype(BF16)
    return _dot(t_bf, a1) + _dot(t_bf, a2) + _dot(t_bf, a3)


def _ssd_chunk(x_ref, b_ref, c_ref, dtc_raw, dtr_raw, bias_c, bias_r, a_c, a_r, state_ref, y_ref, reverse, hp, ns):
    heads = dtc_raw.shape[1]
    groups = b_ref.shape[2] // ns
    nr = heads // groups
    tri, t_mat, t_mat_t = _tri_masks(reverse)
    dt_c = _softplus(dtc_raw + bias_c)
    dt_r = _softplus(dtr_raw + bias_r)
    acs_c = _cumsum_cols(t_mat.astype(BF16), dt_c * a_c) * LOG2E
    acs_r = _dot_exact(dt_r * a_r, t_mat_t) * LOG2E
    end = 0 if reverse else CHUNK - 1
    last_r = acs_r[:, end:end + 1]
    w_r = dt_r * jnp.exp2(last_r - acs_r)
    e_last = jnp.exp2(last_r)
    hpl = LANES // hp
    lane_head = lax.broadcasted_iota(jnp.int32, (1, LANES), 1) // hp
    for g in range(groups):
        bmat = b_ref[0, :, g * ns:(g + 1) * ns]
        cmat = c_ref[0, :, g * ns:(g + 1) * ns]
        cb = _dot_nt(cmat.astype(BF16), bmat.astype(BF16))
        bt = bmat.T
        for t in range(g * nr // hpl, (g + 1) * nr // hpl):
            cols = slice(t * LANES, (t + 1) * LANES)
            x_t = x_ref[0, :, cols].astype(F32)
            st_t = state_ref[t]
            y_t, upd, e_t = None, None, jnp.zeros((1, LANES), F32)
            for e in range(hpl):
                h = t * hpl + e
                own = lane_head == e
                bc = jnp.broadcast_to(acs_c[:, h:h + 1], (CHUNK, CHUNK))
                bc_n = bc if ns == CHUNK else jnp.broadcast_to(acs_c[:, h:h + 1], (CHUNK, ns))
                mp = cb * jnp.exp2(jnp.where(tri, bc - acs_r[h:h + 1, :], NEG_INF)) * dt_r[h:h + 1, :]
                cp = cmat * jnp.exp2(bc_n)
                xh = jnp.where(own, x_t, 0.0).astype(BF16)
                lhs = jnp.concatenate([mp, cp], axis=1).astype(BF16)
                rhs = jnp.concatenate([xh, jnp.where(own, st_t, 0.0).astype(BF16)], axis=0)
                yh = _dot(lhs, rhs)
                uh = _dot((bt * w_r[h:h + 1, :]).astype(BF16), xh)
                y_t = yh if y_t is None else y_t + yh
                upd = uh if upd is None else upd + uh
                e_t = jnp.where(own, e_last[h:h + 1, :], e_t)
            y_ref[0, :, cols] = y_t.astype(y_ref.dtype)
            state_ref[t] = st_t * e_t + upd


def _ssd_scan_kernel(xf_ref, bf_ref, cf_ref, dcf_ref, drf_ref, xb_ref, bb_ref, cb_ref, dcb_ref, drb_ref,
                     biasc_ref, biasr_ref, ac_ref, ar_ref, h0_ref, yf_ref, yb_ref, hout_ref, state_ref, *, hp, ns):
    c = pl.program_id(1)

    @pl.when(c == 0)
    def _():
        state_ref[...] = h0_ref[0]

    _ssd_chunk(xf_ref, bf_ref, cf_ref, dcf_ref[0, 0], drf_ref[0, 0], biasc_ref[0], biasr_ref[0], ac_ref[0], ar_ref[0],
               state_ref.at[0], yf_ref, False, hp, ns)
    _ssd_chunk(xb_ref, bb_ref, cb_ref, dcb_ref[0, 0], drb_ref[0, 0], biasc_ref[1], biasr_ref[1], ac_ref[1], ar_ref[1],
               state_ref.at[1], yb_ref, True, hp, ns)

    @pl.when(c == pl.num_programs(1) - 1)
    def _():
        hout_ref[0] = state_ref[...]


def _ssd_scan(xs, bm, cm, dt_raw, dt_bias, a_log, h0, groups, hp):
    bsz, seq, inner = xs.shape
    heads = inner // hp
    nr = heads // groups
    ns = bm.shape[-1] // groups
    nc = seq // CHUNK
    gn = groups * ns
    dt4 = dt_raw.reshape(bsz, seq, 2, heads)
    dt_col = dt4.transpose(0, 2, 1, 3)
    dt_row = dt4.transpose(0, 2, 3, 1)
    bias = dt_bias.astype(F32).reshape(2, heads)
    a_neg = (-jnp.exp(a_log.astype(F32))).reshape(2, heads)
    hpl = LANES // hp
    nt = heads // hpl
    h0t = h0.reshape(bsz, 2, nt, hpl, hp, ns).transpose(0, 1, 2, 5, 3, 4).reshape(bsz, 2, nt, ns, LANES)

    def fwd(w):
        return pl.BlockSpec((1, CHUNK, w), lambda b, c: (b, c, 0))

    def bwd(w):
        return pl.BlockSpec((1, CHUNK, w), lambda b, c: (b, nc - 1 - c, 0))

    vec_c = _const_spec((2, 1, heads))
    vec_r = _const_spec((2, heads, 1))
    state_spec = pl.BlockSpec((1, 2, nt, ns, LANES), lambda b, c: (b, 0, 0, 0, 0))
    yf, yb, hout = pl.pallas_call(
        functools.partial(_ssd_scan_kernel, hp=hp, ns=ns),
        out_shape=(jax.ShapeDtypeStruct(xs.shape, BF16), jax.ShapeDtypeStruct(xs.shape, BF16),
                   jax.ShapeDtypeStruct(h0t.shape, F32)),
        grid=(bsz, nc),
        in_specs=[fwd(inner), fwd(gn), fwd(gn),
                  pl.BlockSpec((1, 1, CHUNK, heads), lambda b, c: (b, 0, c, 0)),
                  pl.BlockSpec((1, 1, heads, CHUNK), lambda b, c: (b, 0, 0, c)),
                  bwd(inner), bwd(gn), bwd(gn),
                  pl.BlockSpec((1, 1, CHUNK, heads), lambda b, c: (b, 1, nc - 1 - c, 0)),
                  pl.BlockSpec((1, 1, heads, CHUNK), lambda b, c: (b, 1, 0, nc - 1 - c)),
                  vec_c, vec_r, vec_c, vec_r, state_spec],
        out_specs=(fwd(inner), bwd(inner), state_spec),
        scratch_shapes=[pltpu.VMEM((2, nt, ns, LANES), F32)],
        compiler_params=_cparams("parallel", "arbitrary"),
        name="ssd_scan",
    )(xs, bm, cm, dt_col, dt_row, xs, bm, cm, dt_col, dt_row,
      bias[:, None, :], bias[:, :, None], a_neg[:, None, :], a_neg[:, :, None], h0t)
    hout = hout.reshape(bsz, 2, nt, ns, hpl, hp).transpose(0, 1, 2, 4, 5, 3).reshape(bsz, 2, heads, hp, ns)
    return yf, yb, hout


def _ssd_out_prologue(rows, yf_ref, yb_ref, xs_ref, z_ref, d_ref, g_ref):
    y = yf_ref[0, rows].astype(F32) + yb_ref[0, rows].astype(F32) + xs_ref[0, rows].astype(F32) * d_ref[...]
    y = y * _silu(z_ref[0, rows].astype(F32))
    ms = jnp.mean(y * y, axis=-1, keepdims=True)
    return y * lax.rsqrt(ms + EPS) * g_ref[...]


def _diff_attn_kernel(*refs, hd, tk, has_ctx):
    if has_ctx:
        q_ref, k_ref, v_ref, ck_ref, cv_ref, lam_ref, gsub_ref, o_ref, vt_ref = refs
    else:
        q_ref, k_ref, v_ref, lam_ref, gsub_ref, o_ref, vt_ref = refs
    lk = k_ref.shape[1]
    blocks = [(k_ref, j * tk, j * tk, tk) for j in range(lk // tk)]
    if has_ctx:
        blocks.append((ck_ref, 0, lk, ck_ref.shape[1]))

    @pl.when(pl.program_id(2) == 0)
    def _():
        for j in range(lk // tk):
            vt_ref[:, j * tk:(j + 1) * tk] = v_ref[0, j * tk:(j + 1) * tk, :].astype(F32).T.astype(BF16)
        if has_ctx:
            vt_ref[:, lk:] = cv_ref[0].astype(F32).T.astype(BF16)

    q = q_ref[0].astype(F32)
    lane = lax.broadcasted_iota(jnp.int32, q.shape, 1)
    qz = [jnp.where(lane < hd, q, 0.0).astype(BF16), jnp.where(lane >= hd, q, 0.0).astype(BF16)]
    m, l, acc = [None, None], [None, None], [None, None]
    items = [(j, c) for j in range(len(blocks)) for c in range(2)]

    def scores(j, c):
        ref, r0, _, n = blocks[j]
        return _dot_nt(ref[0, r0:r0 + n, :], qz[c])

    ahead = [scores(*it) for it in items[:SCORE_LOOKAHEAD]]
    for idx, (j, c) in enumerate(items):
        st = ahead.pop(0)
        if idx + SCORE_LOOKAHEAD < len(items):
            ahead.append(scores(*items[idx + SCORE_LOOKAHEAD]))
        _, _, c0, n = blocks[j]
        vtb = vt_ref[:, c0:c0 + n]
        mx = jnp.max(st, axis=0, keepdims=True)
        if j == 0:
            m[c] = mx
            p = jnp.exp2(st - mx)
            l[c] = jnp.sum(p, axis=0, keepdims=True)
            acc[c] = _dot(vtb, p.astype(BF16))
        else:
            m_new = jnp.maximum(m[c], mx)
            alpha = jnp.exp2(m[c] - m_new)
            p = jnp.exp2(st - m_new)
            l[c] = alpha * l[c] + jnp.sum(p, axis=0, keepdims=True)
            acc[c] = alpha * acc[c] + _dot(vtb, p.astype(BF16))
            m[c] = m_new
    lv = lam_ref[...]
    f1 = jnp.exp(jnp.sum(lv[0:1] * lv[1:2], axis=1, keepdims=True))
    f2 = jnp.exp(jnp.sum(lv[2:3] * lv[3:4], axis=1, keepdims=True))
    lam = f1 - f2 + DIFF_LAMBDA_INIT
    o = (acc[0] / l[0] - lam * (acc[1] / l[1])).T
    ms = jnp.mean(o * o, axis=-1, keepdims=True)
    o_ref[0] = (o * lax.rsqrt(ms + EPS) * gsub_ref[...] * (1.0 - DIFF_LAMBDA_INIT)).astype(o_ref.dtype)


def _diff_attn(q, k, v, ctx, lam_vecs, g_sub, heads, hd, tq, tk):
    bsz, lq, n = q.shape
    lk = k.shape[1]
    vd = 2 * hd
    kv_spec = pl.BlockSpec((1, lk, vd), lambda b, h, i: (b, 0, h))
    ins, specs, ltot = [q, k, v], [pl.BlockSpec((1, tq, vd), lambda b, h, i: (b, i, h)), kv_spec, kv_spec], lk
    if ctx is not None:
        lc = ctx[0].shape[1]
        ins += list(ctx)
        specs += [pl.BlockSpec((1, lc, vd), lambda b, h, i: (b, 0, h))] * 2
        ltot += lc
    return pl.pallas_call(
        functools.partial(_diff_attn_kernel, hd=hd, tk=tk, has_ctx=ctx is not None),
        out_shape=jax.ShapeDtypeStruct((bsz, lq, n), BF16),
        grid=(bsz, heads, lq // tq),
        in_specs=specs + [pl.BlockSpec(lam_vecs.shape, lambda b, h, i: (0, 0)),
                          pl.BlockSpec((1, vd), lambda b, h, i: (0, 0))],
        out_specs=pl.BlockSpec((1, tq, vd), lambda b, h, i: (b, i, h)),
        scratch_shapes=[pltpu.VMEM((vd, ltot), BF16)],
        compiler_params=_cparams("parallel", "parallel", "arbitrary"),
        name="diff_attn",
    )(*ins, lam_vecs, g_sub[None])


def _ml_proj_kernel(x_ref, xp_ref, xn_ref, a_ref, sh_ref, wm_ref, wo_ref, cw_ref, cb_ref, wq_ref, wk_ref, wv_ref,
                    wif_ref, bif_ref, q_ref, k_ref, v_ref, g_ref, op_ref, xe_ref, *, tm, nl, heads, ihd, dk):
    xe_ref[...] = _ext_rows(x_ref, xp_ref, xn_ref, a_ref[0], sh_ref[0], nl).astype(BF16)
    def in_proj(h):
        return _dot(xe_ref[...], wm_ref[:, h * ihd:(h + 1) * ihd])

    nxt = in_proj(0)
    op_ref[0] = _dot(xe_ref[HALO:HALO + tm, :], wo_ref[...]).astype(op_ref.dtype)
    gates = jnp.zeros((tm, bif_ref.shape[1]), F32) + bif_ref[...]
    for h in range(heads):
        sl = slice(h * ihd, (h + 1) * ihd)
        xm_e = nxt
        if h + 1 < heads:
            nxt = in_proj(h + 1)
        xc = _silu(_dwconv_rows(xm_e, cw_ref[:, sl], cb_ref[:, sl], 5))[HALO:HALO + tm].astype(BF16)
        xm = xm_e[HALO:HALO + tm].astype(BF16)
        q_ref[0, :, h * dk:(h + 1) * dk] = _dot(xc, wq_ref[h]).astype(q_ref.dtype)
        k_ref[0, :, h * dk:(h + 1) * dk] = (_dot(xc, wk_ref[h]) * (dk ** -0.5)).astype(k_ref.dtype)
        v_ref[0, :, sl] = _dot(xm, wv_ref[h]).astype(v_ref.dtype)
        gates = gates + _dot(xc, wif_ref[sl, :])
    g_ref[0] = gates


def _ml_project(x, a, sh, w_in, conv_w, conv_b, w_q, w_k, w_v, w_if, b_if, tm):
    bsz, seq, d = x.shape
    heads, ihd, dk = w_q.shape
    inner = heads * ihd
    nl = seq // tm
    ng = w_if.shape[1]
    wm = w_in[:, :inner].astype(BF16)
    wo = w_in[:, inner:].astype(BF16)
    mod_spec = pl.BlockSpec((1, 1, d), lambda b, l: (b, 0, 0))
    sds = lambda n, dt: jax.ShapeDtypeStruct((bsz, seq, n), dt)
    return pl.pallas_call(
        functools.partial(_ml_proj_kernel, tm=tm, nl=nl, heads=heads, ihd=ihd, dk=dk),
        out_shape=(sds(heads * dk, BF16), sds(heads * dk, BF16), sds(inner, BF16), sds(ng, F32), sds(inner, BF16)),
        grid=(bsz, nl),
        in_specs=_halo_specs(tm, d, nl) + [mod_spec, mod_spec, _const_spec(wm.shape), _const_spec(wo.shape),
                                           _const_spec(conv_w.shape), _const_spec((1, inner)),
                                           _const_spec(w_q.shape), _const_spec(w_k.shape), _const_spec(w_v.shape),
                                           _const_spec(w_if.shape), _const_spec((1, ng))],
        out_specs=(_row_spec(tm, heads * dk), _row_spec(tm, heads * dk), _row_spec(tm, inner), _row_spec(tm, ng),
                   _row_spec(tm, inner)),
        scratch_shapes=[pltpu.VMEM((tm + 2 * HALO, d), BF16)],
        compiler_params=_cparams("parallel", "parallel"),
        name="ml_proj",
    )(x, x, x, a, sh, wm, wo, conv_w, conv_b[None], w_q.astype(BF16), w_k.astype(BF16), w_v.astype(BF16),
      w_if.astype(BF16), b_if.reshape(1, ng))


def _ml_gate_free(q_ref, k_ref, c_ref):
    heads, dk, _ = c_ref.shape
    qk, qc = [], []
    for hh in range(heads):
        q = q_ref[0, :, hh * dk:(hh + 1) * dk]
        qk.append(_dot_nt(q, k_ref[0, :, hh * dk:(hh + 1) * dk]))
        qc.append(_dot(q, c_ref[hh].astype(BF16)))
    return qk, qc


def _ml_gate_cumsums(gc, gr, reverse):
    _, t_mat, t_mat_t = _tri_masks(reverse)
    return _cumsum_cols(t_mat.astype(BF16), _log_sigmoid(gc)), _dot_exact(_log_sigmoid(gr), t_mat_t)


def _ml_chunk(q_ref, k_ref, v_ref, gc, gr, d, c_ref, n_ref, m_ref, h_ref, reverse, bcum_call, bcum_rall, qk, qc):
    heads, dk, dv = c_ref.shape
    tri = _tri_masks(reverse)[0]
    end = 0 if reverse else CHUNK - 1
    for hh in range(heads):
        ci, cf = d * 2 * heads + hh, d * 2 * heads + heads + hh
        q = q_ref[0, :, hh * dk:(hh + 1) * dk]
        k = k_ref[0, :, hh * dk:(hh + 1) * dk]
        v = v_ref[0, :, hh * dv:(hh + 1) * dv]
        bcum_c = bcum_call[:, cf:cf + 1]
        bcum_r = bcum_rall[cf:cf + 1, :]
        i_c = gc[:, ci:ci + 1]
        i_r = gr[ci:ci + 1, :]
        m_old = m_ref[hh]
        cmat = c_ref[hh]
        nrow = n_ref[hh]
        dmat = jnp.where(tri, bcum_c - bcum_r + i_r, NEG_INF)
        inter = bcum_c + m_old
        mt = jnp.maximum(jnp.max(dmat, axis=1, keepdims=True), inter)
        s = qk[hh] * jnp.exp(dmat - mt)
        w_int = jnp.exp(inter - mt)
        num = _dot(s.astype(BF16), v) + w_int * qc[hh]
        den = jnp.sum(s, axis=1, keepdims=True) + w_int * jnp.sum(q.astype(F32) * nrow, axis=1, keepdims=True)
        h_ref[0, :, hh * dv:(hh + 1) * dv] = (num / jnp.maximum(jnp.abs(den), jnp.exp(-mt))).astype(h_ref.dtype)
        btot = bcum_c[end:end + 1]
        g = btot - bcum_c + i_c
        m_new = jnp.maximum(btot + m_old, jnp.max(g, axis=0, keepdims=True))
        decay = jnp.exp(btot + m_old - m_new)
        kw = k.astype(F32) * jnp.exp(g - m_new)
        c_ref[hh] = cmat * decay + _dot(kw.T.astype(BF16), v)
        n_ref[hh] = nrow * decay + jnp.sum(kw, axis=0, keepdims=True)
        m_ref[hh] = m_new


def _ml_scan_kernel(qf_ref, kf_ref, vf_ref, gcf_ref, grf_ref, qb_ref, kb_ref, vb_ref, gcb_ref, grb_ref,
                    c0_ref, n0_ref, m0_ref, hf_ref, hb_ref, cout_ref, nout_ref, mout_ref, c_ref, n_ref, m_ref):
    c = pl.program_id(1)

    @pl.when(c == 0)
    def _():
        c_ref[...] = c0_ref[0]
        n_ref[...] = n0_ref[0]
        m_ref[...] = m0_ref[0]

    pre_f = _ml_gate_free(qf_ref, kf_ref, c_ref.at[0])
    pre_b = _ml_gate_free(qb_ref, kb_ref, c_ref.at[1])
    cum_f = _ml_gate_cumsums(gcf_ref[0], grf_ref[0], False)
    _ml_chunk(qf_ref, kf_ref, vf_ref, gcf_ref[0], grf_ref[0], 0, c_ref.at[0], n_ref.at[0], m_ref.at[0], hf_ref, False,
              *cum_f, *pre_f)
    cum_b = _ml_gate_cumsums(gcb_ref[0], grb_ref[0], True)
    _ml_chunk(qb_ref, kb_ref, vb_ref, gcb_ref[0], grb_ref[0], 1, c_ref.at[1], n_ref.at[1], m_ref.at[1], hb_ref, True,
              *cum_b, *pre_b)

    @pl.when(c == pl.num_programs(1) - 1)
    def _():
        cout_ref[0] = c_ref[...]
        nout_ref[0] = n_ref[...]
        mout_ref[0] = m_ref[...]


def _ml_scan(q, k, v, gates, c0, n0, m0):
    bsz, seq, _ = q.shape
    _, _, heads, dk, dv = c0.shape
    ng = gates.shape[-1]
    nc = seq // CHUNK
    g_row = gates.transpose(0, 2, 1)
    n0 = n0.reshape(bsz, 2, heads, 1, dk)
    m0 = m0.reshape(bsz, 2, heads, 1, 1)

    def fwd(w):
        return pl.BlockSpec((1, CHUNK, w), lambda b, c: (b, c, 0))

    def bwd(w):
        return pl.BlockSpec((1, CHUNK, w), lambda b, c: (b, nc - 1 - c, 0))

    c_spec = pl.BlockSpec((1, 2, heads, dk, dv), lambda b, c: (b, 0, 0, 0, 0))
    n_spec = pl.BlockSpec((1, 2, heads, 1, dk), lambda b, c: (b, 0, 0, 0, 0))
    m_spec = pl.BlockSpec((1, 2, heads, 1, 1), lambda b, c: (b, 0, 0, 0, 0))
    hf, hb, cout, nout, mout = pl.pallas_call(
        _ml_scan_kernel,
        out_shape=(jax.ShapeDtypeStruct(v.shape, BF16), jax.ShapeDtypeStruct(v.shape, BF16),
                   jax.ShapeDtypeStruct(c0.shape, F32), jax.ShapeDtypeStruct(n0.shape, F32),
                   jax.ShapeDtypeStruct(m0.shape, F32)),
        grid=(bsz, nc),
        in_specs=[fwd(heads * dk), fwd(heads * dk), fwd(heads * dv), fwd(ng),
                  pl.BlockSpec((1, ng, CHUNK), lambda b, c: (b, 0, c)),
                  bwd(heads * dk), bwd(heads * dk), bwd(heads * dv), bwd(ng),
                  pl.BlockSpec((1, ng, CHUNK), lambda b, c: (b, 0, nc - 1 - c)),
                  c_spec, n_spec, m_spec],
        out_specs=(fwd(heads * dv), bwd(heads * dv), c_spec, n_spec, m_spec),
        scratch_shapes=[pltpu.VMEM((2, heads, dk, dv), F32), pltpu.VMEM((2, heads, 1, dk), F32),
                        pltpu.VMEM((2, heads, 1, 1), F32)],
        compiler_params=_cparams("parallel", "arbitrary"),
        name="ml_scan",
    )(q, k, v, gates, g_row, q, k, v, gates, g_row, c0, n0, m0)
    return hf, hb, cout, nout.reshape(bsz, 2, heads, dk), mout.reshape(bsz, 2, heads)


def _ml_out_prologue(rows, hf_ref, hb_ref, op_ref, g_ref, *, heads, dv):
    h = hf_ref[0, rows].astype(F32) + hb_ref[0, rows].astype(F32)
    outs = []
    for hh in range(heads):
        blk = h[:, hh * dv:(hh + 1) * dv]
        ms = jnp.mean(blk * blk, axis=-1, keepdims=True)
        outs.append(blk * lax.rsqrt(ms + EPS))
    return jax.nn.sigmoid(op_ref[0, rows].astype(F32)) * (jnp.concatenate(outs, axis=1) * g_ref[...])


def _stream_mods(mod, g1, g2):
    a1 = (g1 * (1.0 + mod[:, 1]))[:, None]
    a2 = (g2 * (1.0 + mod[:, 4]))[:, None]
    return (a1, mod[:, 0][:, None], mod[:, 2][:, None]), (a2, mod[:, 3][:, None], mod[:, 5][:, None])


def _tile(seq):
    return min(seq, 512)


def _attn_layer(x, mods, lat_cache, w_qkv, g_q, g_k, sink, w_o):
    a, sh, gt = mods
    bsz, seq, d = x.shape
    hd = g_q.shape[0]
    heads = sink.shape[0]
    kvh = (w_qkv.shape[1] // hd - heads) // 2
    tm = _tile(seq)
    gq = jnp.tile(g_q, LANES // hd)[None]
    gk = jnp.tile(g_k, LANES // hd)[None]
    qscale = hd ** -0.5 * LOG2E
    if lat_cache is None:
        q, k, v = _qkv_project(x, a, sh, w_qkv, gq, gk, heads * hd, kvh * hd, kvh * hd, None, F32, tm, qscale)
        o = _attn_ctx(q, k, v, sink, kvh, heads // kvh, hd)
    else:
        ck, cv = lat_cache
        q, k, v = _qkv_project(x, a, sh, w_qkv, gq, gk, heads * hd, kvh * hd, kvh * hd, _rope_tables(seq, hd), BF16, tm,
                               qscale)
        o = _attn_lat(q, k, v, ck.reshape(bsz, -1, kvh * hd), cv.reshape(bsz, -1, kvh * hd), sink, kvh,
                      heads // kvh, hd, 128)
    x = _oproj(_rows_prologue, [o],[_row_spec(tm, heads * hd)], x, gt, w_o, tm)
    return x, k.reshape(bsz, seq, kvh, hd), v.reshape(bsz, seq, kvh, hd)


def _ssd_layer(x, mods, h0, w_in, conv_w, conv_b, dt_bias, a_log, d_skip, g_norm, w_out):
    a, sh, gt = mods
    bsz, seq, d = x.shape
    heads = d_skip.shape[0]
    inner = g_norm.shape[0]
    hp = inner // heads
    gs = (conv_w.shape[1] - inner) // 2
    groups = gs // LANES
    tm = _tile(seq)
    z, xs, bm, cm, dt_raw = _ssd_project(x, a, sh, w_in, conv_w, conv_b, inner, gs, 2 * heads, tm)
    yf, yb, hout = _ssd_scan(xs, bm, cm, dt_raw, dt_bias, a_log, h0, groups, hp)
    rs = _row_spec(tm, inner)
    x = _oproj(_ssd_out_prologue, [yf, yb, xs, z, jnp.repeat(d_skip, hp)[None], g_norm[None]],
               [rs, rs, rs, rs, _const_spec((1, inner)), _const_spec((1, inner))], x, gt, w_out, tm)
    return x, hout


def _diff_layer(x, mods, lat_cache, w_qkv, g_q, g_k, lam_vecs, g_sub, w_o):
    a, sh, gt = mods
    bsz, seq, d = x.shape
    hd = g_q.shape[1]
    vd = g_sub.shape[0]
    heads = w_qkv.shape[1] // (4 * hd + vd)
    nq = heads * 2 * hd
    tm = _tile(seq)
    gq = g_q.reshape(1, 2 * hd)
    gk = g_k.reshape(1, 2 * hd)
    qscale = hd ** -0.5 * math.log2(math.e)
    if lat_cache is None:
        q, k, v = _qkv_project(x, a, sh, w_qkv, gq, gk, nq, nq, heads * vd, None, F32, tm, qscale)
        o = _diff_attn(q, k.astype(BF16), v.astype(BF16), None, lam_vecs, g_sub, heads, hd, seq, seq)
    else:
        ck, cv = lat_cache
        q, k, v = _qkv_project(x, a, sh, w_qkv, gq, gk, nq, nq, heads * vd, _rope_tables(seq, hd), BF16, tm, qscale)
        ctx = (ck.reshape(bsz, -1, nq).astype(BF16), cv.reshape(bsz, -1, heads * vd).astype(BF16))
        o = _diff_attn(q, k, v, ctx, lam_vecs, g_sub, heads, hd, tm, min(seq, 1024))
    x = _oproj(_rows_prologue, [o],[_row_spec(tm, heads * vd)], x, gt, w_o, tm)
    return x, k.reshape(bsz, seq, heads, 2, hd), v.reshape(bsz, seq, heads, vd)


def _ml_layer(x, mods, c0, n0, m0, w_in, conv_w, conv_b, w_q, w_k, w_v, w_if, b_if, g_norm, w_out):
    a, sh, gt = mods
    bsz, seq, d = x.shape
    heads, dv = g_norm.shape
    inner = heads * dv
    tm = _tile(seq)
    q, k, v, gates, o_pre = _ml_project(x, a, sh, w_in, conv_w, conv_b, w_q, w_k, w_v, w_if, b_if, tm)
    hf, hb, cout, nout, mout = _ml_scan(q, k, v, gates, c0, n0, m0)
    rs = _row_spec(tm, inner)
    x = _oproj(functools.partial(_ml_out_prologue, heads=heads, dv=dv), [hf, hb, o_pre, g_norm.reshape(1, inner)],
               [rs, rs, rs, _const_spec((1, inner))], x, gt, w_out, tm)
    return x, cout, nout, mout


def kernel(x_prompt, x_sample, cache_attn_k, cache_attn_v, state_ssd, cache_diff_k, cache_diff_v, state_mlstm_c, state_mlstm_n, state_mlstm_m, c, c_ctx, ada_w, ada_b, norm1_g, norm2_g, ffn_w_up, ffn_conv_w, ffn_conv_b, ffn_w_down, attn_w_qkv, attn_g_q, attn_g_k, attn_sink, attn_w_o, ssd_w_in, ssd_conv_w, ssd_conv_b, ssd_dt_bias, ssd_a_log, ssd_d, ssd_g_norm, ssd_w_out, diff_w_qkv, diff_g_q, diff_g_k, diff_lq1, diff_lk1, diff_lq2, diff_lk2, diff_g_sub, diff_w_o, ml_w_in, ml_conv_w, ml_conv_b, ml_w_q, ml_w_k, ml_w_v, ml_w_if, ml_b_if, ml_g_norm, ml_w_out):
    xp, xs = x_prompt, x_sample
    bp, bs = xp.shape[0], xs.shape[0]
    d = xp.shape[-1]
    depth = ada_w.shape[0]
    rows = 16
    cond = jnp.concatenate([c_ctx[None], c, jnp.zeros((rows - 1 - bs, d), F32)], axis=0)
    mod = _ada(cond, ada_w, ada_b).reshape(depth, rows, 6, d)
    lam_vecs = jnp.stack([diff_lq1, diff_lk1, diff_lq2, diff_lk2])
    outs = {}
    for i in range(depth):
        mp1, mp2 = _stream_mods(jnp.broadcast_to(mod[i, 0], (bp, 6, d)), norm1_g[i], norm2_g[i])
        ms1, ms2 = _stream_mods(mod[i, 1:1 + bs], norm1_g[i], norm2_g[i])
        kind = i % 4
        if kind == 0:
            xp, outs["ak"], outs["av"] = _attn_layer(xp, mp1, None, attn_w_qkv, attn_g_q, attn_g_k, attn_sink, attn_w_o)
            xs, _, _ = _attn_layer(xs, ms1, (cache_attn_k, cache_attn_v), attn_w_qkv, attn_g_q, attn_g_k, attn_sink,
                                   attn_w_o)
        elif kind == 1:
            ssd_w = (ssd_w_in, ssd_conv_w, ssd_conv_b, ssd_dt_bias, ssd_a_log, ssd_d, ssd_g_norm, ssd_w_out)
            xp, outs["ssd"] = _ssd_layer(xp, mp1, jnp.zeros((bp,) + state_ssd.shape[1:], F32), *ssd_w)
            xs, _ = _ssd_layer(xs, ms1, state_ssd, *ssd_w)
        elif kind == 2:
            diff_w = (diff_w_qkv, diff_g_q, diff_g_k, lam_vecs, diff_g_sub, diff_w_o)
            xp, outs["dk"], outs["dv"] = _diff_layer(xp, mp1, None, *diff_w)
            xs, _, _ = _diff_layer(xs, ms1, (cache_diff_k, cache_diff_v), *diff_w)
        else:
            ml_w = (ml_w_in, ml_conv_w, ml_conv_b, ml_w_q, ml_w_k, ml_w_v, ml_w_if, ml_b_if, ml_g_norm, ml_w_out)
            zc = jnp.zeros((bp,) + state_mlstm_c.shape[1:], F32)
            zn = jnp.zeros((bp,) + state_mlstm_n.shape[1:], F32)
            zm = jnp.zeros((bp,) + state_mlstm_m.shape[1:], F32)
            xp, outs["mc"], outs["mn"], outs["mm"] = _ml_layer(xp, mp1, zc, zn, zm, *ml_w)
            xs, _, _, _ = _ml_layer(xs, ms1, state_mlstm_c, state_mlstm_n, state_mlstm_m, *ml_w)
        ffn_w = (ffn_w_up[i], ffn_conv_w[i], ffn_conv_b[i], ffn_w_down[i])
        xp = _ffn(xp, *mp2, *ffn_w, tm=_tile(xp.shape[1]))
        xs = _ffn(xs, *ms2, *ffn_w, tm=_tile(xs.shape[1]))
    return (xp, xs, outs["ak"], outs["av"], outs["ssd"], outs["dk"], outs["dv"], outs["mc"], outs["mn"], outs["mm"])
```
